```python
import math
import jax
import jax.numpy as jnp
from jax import lax
import numpy as np


D_MODEL = 1024
BATCH = 4
SEQ = 4096
DEPTH = 4
DEC_BATCH = 128
DEC_SEQ = 1
PAST_LEN = 8192
PAGE_SIZE = 128

N_EVEN = (DEPTH + 1) // 2
N_ODD = DEPTH // 2
EPS = 1e-6
ROPE_THETA = 10000.0
Q_BLOCK = 128

GLA_HEADS = 4
GLA_DK = 64
GLA_DV = 128
GLA_QK_W = GLA_HEADS * GLA_DK
GLA_V_W = GLA_HEADS * GLA_DV
GLA_GATE_RANK = 16
GLA_GATE_TEMP = 16.0
GLA_CHUNK = 64

MLA_HEADS = 4
MLA_Q_RANK = 384
MLA_KV_RANK = 256
MLA_NOPE_DIM = 128
MLA_ROPE_DIM = 64
MLA_V_DIM = 128

DIFF_HEADS = 8
DIFF_KV_HEADS = 2
DIFF_REP = DIFF_HEADS // DIFF_KV_HEADS
DIFF_DH = 64
DIFF_DV = 2 * DIFF_DH
DIFF_Q_W = DIFF_HEADS * 2 * DIFF_DH
DIFF_K_W = DIFF_KV_HEADS * 2 * DIFF_DH
DIFF_V_W = DIFF_KV_HEADS * DIFF_DV

EVEN_IN = 2 * GLA_QK_W + GLA_V_W + GLA_GATE_RANK + GLA_V_W + MLA_Q_RANK + MLA_KV_RANK + MLA_ROPE_DIM
EVEN_MIX = GLA_V_W + MLA_HEADS * MLA_V_DIM
ODD_IN = DIFF_Q_W + DIFF_K_W + DIFF_V_W
ODD_MIX = DIFF_HEADS * DIFF_DV

D_FF = -(-8 * D_MODEL // (3 * 256)) * 256

kernel_name = 'hybrid_gla_mla_diffattn_decoder_step'


def rms_norm(x, g):
    xf = x.astype(jnp.float32)
    y = xf * lax.rsqrt(jnp.mean(xf * xf, axis=-1, keepdims=True) + EPS)
    return (y * g.astype(jnp.float32)).astype(x.dtype)


def rope(x, pos):
    d = x.shape[-1]
    inv = ROPE_THETA ** (-jnp.arange(0, d, 2, dtype=jnp.float32) / d)
    ang = pos.astype(jnp.float32)[:, None] * inv[None, :]
    ang = ang.reshape(ang.shape[:1] + (1,) * (x.ndim - 3) + ang.shape[1:])
    cos, sin = jnp.cos(ang), jnp.sin(ang)
    xf = x.astype(jnp.float32)
    x1, x2 = xf[..., : d // 2], xf[..., d // 2:]
    return jnp.concatenate([x1 * cos - x2 * sin, x2 * cos + x1 * sin], axis=-1).astype(x.dtype)


def gather_pages(pool, page_table):
    g = pool[page_table]
    return g.reshape((page_table.shape[0], page_table.shape[1] * PAGE_SIZE) + pool.shape[2:])


def sweep_query_blocks(block_fn, q_arrays, l):
    nb = l // Q_BLOCK

    def body(i):
        start = i * Q_BLOCK
        qs = [lax.dynamic_slice_in_dim(a, start, Q_BLOCK, axis=1) for a in q_arrays]
        qpos = start + jnp.arange(Q_BLOCK, dtype=jnp.int32)
        return block_fn(qs, qpos)

    out = lax.map(body, jnp.arange(nb, dtype=jnp.int32))
    out = jnp.moveaxis(out, 0, 1)
    return out.reshape((q_arrays[0].shape[0], l) + out.shape[3:])


def joint_softmax(s_past, s_new):
    l = s_new.shape[-1]
    causal = jnp.tril(jnp.ones((l, l), dtype=bool))
    s_new = jnp.where(causal, s_new, -jnp.inf)
    p = jax.nn.softmax(jnp.concatenate([s_past, s_new], axis=-1), axis=-1)
    n_past = s_past.shape[-1]
    return p[..., :n_past], p[..., n_past:]


def gla_chunked(q, k, v, log_a, s0):
    b, l, h, _ = q.shape
    c = min(GLA_CHUNK, l)
    n = -(-l // c)
    pad = n * c - l

    def prep(t):
        t = jnp.pad(t.astype(jnp.float32), ((0, 0), (0, pad), (0, 0), (0, 0)))
        return t.reshape(b, n, c, h, t.shape[-1]).transpose(1, 0, 3, 2, 4)

    qc, kc, vc, gc = prep(q), prep(k), prep(v), prep(log_a)
    causal = jnp.tril(jnp.ones((c, c), dtype=bool))

    def step(s, inp):
        qi, ki, vi, gi = inp
        cum = jnp.cumsum(gi, axis=2)
        diff = cum[:, :, :, None, :] - cum[:, :, None, :, :]
        decay = jnp.exp(jnp.where(causal[:, :, None], diff, -jnp.inf))
        att = jnp.einsum('bhtd,bhsd,bhtsd->bhts', qi, ki, decay)
        o = jnp.einsum('bhts,bhsv->bhtv', att, vi) + jnp.einsum('bhtd,bhdv->bhtv', qi * jnp.exp(cum), s)
        last = cum[:, :, -1:, :]
        s_new = jnp.exp(last[:, :, 0, :])[..., None] * s + jnp.einsum('bhsd,bhsv->bhdv', ki * jnp.exp(last - cum), vi)
        return s_new, o

    s_fin, o = lax.scan(step, s0.astype(jnp.float32), (qc, kc, vc, gc))
    o = o.transpose(1, 0, 3, 2, 4).reshape(b, n * c, h, -1)[:, :l]
    return o, s_fin


def mla_prompt_attention(q_lat, q_rope, c_kv, k_rope):
    l = q_lat.shape[1]
    scale = (MLA_NOPE_DIM + MLA_ROPE_DIM) ** -0.5
    kpos = jnp.arange(l, dtype=jnp.int32)

    def block(qs, qpos):
        qb, qr = qs
        s = jnp.einsum('bqhr,bkr->bhqk', qb, c_kv) + jnp.einsum('bqhe,bke->bhqk', qr, k_rope)
        s = jnp.where(kpos[None, :] <= qpos[:, None], s.astype(jnp.float32) * scale, -jnp.inf)
        p = jax.nn.softmax(s, axis=-1).astype(c_kv.dtype)
        return jnp.einsum('bhqk,bkr->bqhr', p, c_kv)

    return sweep_query_blocks(block, [q_lat, q_rope], l)


def mla_sample_attention(q_lat, q_rope, c_kv, k_rope, ckv_past, krope_past):
    scale = (MLA_NOPE_DIM + MLA_ROPE_DIM) ** -0.5
    s_past = jnp.einsum('bqhr,bkr->bhqk', q_lat, ckv_past) + jnp.einsum('bqhe,bke->bhqk', q_rope, krope_past)
    s_new = jnp.einsum('bqhr,bkr->bhqk', q_lat, c_kv) + jnp.einsum('bqhe,bke->bhqk', q_rope, k_rope)
    p_past, p_new = joint_softmax(s_past.astype(jnp.float32) * scale, s_new.astype(jnp.float32) * scale)
    return (jnp.einsum('bhqk,bkr->bqhr', p_past.astype(c_kv.dtype), ckv_past)
            + jnp.einsum('bhqk,bkr->bqhr', p_new.astype(c_kv.dtype), c_kv))


def even_mixer(h, pos, gla_s0, mla_past, w_in, w_alpha, b_alpha, gla_norm, q_norm, w_uq, kv_norm, w_uk, w_uv, w_out):
    b, l, _ = h.shape
    widths = [GLA_QK_W, GLA_QK_W, GLA_V_W, GLA_GATE_RANK, GLA_V_W, MLA_Q_RANK, MLA_KV_RANK, MLA_ROPE_DIM]
    idx = [int(v) for v in np.cumsum(widths)[:-1]]
    q_g, k_g, v_g, a_lr, r_g, c_q_raw, ckv_raw, kr_raw = jnp.split(h @ w_in, idx, axis=-1)
    qg = q_g.reshape(b, l, GLA_HEADS, GLA_DK) * (GLA_DK ** -0.5)
    kg = k_g.reshape(b, l, GLA_HEADS, GLA_DK)
    vg = v_g.reshape(b, l, GLA_HEADS, GLA_DV)
    log_a = jax.nn.log_sigmoid((a_lr @ w_alpha + b_alpha).astype(jnp.float32)) / GLA_GATE_TEMP
    log_a = log_a.reshape(b, l, GLA_HEADS, GLA_DK)
    o_g, s_fin = gla_chunked(qg, kg, vg, log_a, gla_s0)
    o_g = rms_norm(o_g, gla_norm).astype(h.dtype) * jax.nn.silu(r_g.reshape(b, l, GLA_HEADS, GLA_DV))
    o_g = o_g.reshape(b, l, GLA_V_W)
    c_q = rms_norm(c_q_raw, q_norm)
    q = (c_q @ w_uq).reshape(b, l, MLA_HEADS, MLA_NOPE_DIM + MLA_ROPE_DIM)
    q_nope = q[..., :MLA_NOPE_DIM]
    q_rope = rope(q[..., MLA_NOPE_DIM:], pos)
    c_kv = rms_norm(ckv_raw, kv_norm)
    k_rope = rope(kr_raw, pos)
    q_lat = jnp.einsum('blhn,rhn->blhr', q_nope, w_uk)
    if mla_past is None:
        o_lat = mla_prompt_attention(q_lat, q_rope, c_kv, k_rope)
    else:
        o_lat = mla_sample_attention(q_lat, q_rope, c_kv, k_rope, mla_past[0], mla_past[1])
    o_m = jnp.einsum('blhr,rhv->blhv', o_lat, w_uv).reshape(b, l, MLA_HEADS * MLA_V_DIM)
    out = jnp.concatenate([o_g, o_m], axis=-1) @ w_out
    return out, s_fin, c_kv, k_rope


def diff_prompt_attention(q, k, v, lam):
    l = q.shape[1]
    scale = DIFF_DH ** -0.5
    kpos = jnp.arange(l, dtype=jnp.int32)

    def block(qs, qpos):
        (qb,) = qs
        s = jnp.einsum('bqgrcd,bkgcd->bgrcqk', qb, k).astype(jnp.float32) * scale
        s = jnp.where(kpos[None, :] <= qpos[:, None], s, -jnp.inf)
        p = jax.nn.softmax(s, axis=-1)
        a = (p[:, :, :, 0] - lam * p[:, :, :, 1]).astype(v.dtype)
        return jnp.einsum('bgrqk,bkgv->bqgrv', a, v)

    return sweep_query_blocks(block, [q], l)


def diff_sample_attention(q, k, v, lam, k_past, v_past):
    b = q.shape[0]
    scale = DIFF_DH ** -0.5
    kp = k_past.reshape(b, -1, DIFF_KV_HEADS, 2, DIFF_DH)
    s_past = jnp.einsum('bqgrcd,bkgcd->bgrcqk', q, kp).astype(jnp.float32) * scale
    s_new = jnp.einsum('bqgrcd,bkgcd->bgrcqk', q, k).astype(jnp.float32) * scale
    p_past, p_new = joint_softmax(s_past, s_new)
    a_past = (p_past[:, :, :, 0] - lam * p_past[:, :, :, 1]).astype(v.dtype)
    a_new = (p_new[:, :, :, 0] - lam * p_new[:, :, :, 1]).astype(v.dtype)
    return jnp.einsum('bgrqk,bkgv->bqgrv', a_past, v_past) + jnp.einsum('bgrqk,bkgv->bqgrv', a_new, v)


def odd_mixer(h, pos, diff_past, layer_idx, w_in, lq1, lk1, lq2, lk2, subln, w_out):
    b, l, _ = h.shape
    q, k, v = jnp.split(h @ w_in, [DIFF_Q_W, DIFF_Q_W + DIFF_K_W], axis=-1)
    q = rope(q.reshape(b, l, DIFF_HEADS, 2, DIFF_DH), pos).reshape(b, l, DIFF_KV_HEADS, DIFF_REP, 2, DIFF_DH)
    k = rope(k.reshape(b, l, DIFF_KV_HEADS, 2, DIFF_DH), pos)
    v = v.reshape(b, l, DIFF_KV_HEADS, DIFF_DV)
    lam_init = 0.8 - 0.6 * math.exp(-0.3 * layer_idx)
    lam = (jnp.exp(jnp.sum(lq1.astype(jnp.float32) * lk1.astype(jnp.float32)))
           - jnp.exp(jnp.sum(lq2.astype(jnp.float32) * lk2.astype(jnp.float32))) + lam_init)
    if diff_past is None:
        o = diff_prompt_attention(q, k, v, lam)
    else:
        o = diff_sample_attention(q, k, v, lam, diff_past[0], diff_past[1])
    o = rms_norm(o.reshape(b, l, DIFF_HEADS, DIFF_DV), subln) * (1.0 - lam_init)
    out = o.reshape(b, l, ODD_MIX) @ w_out
    return out, k.reshape(b, l, DIFF_KV_HEADS, 2 * DIFF_DH), v


def swiglu(h, w_gate, w_up, w_down):
    return (jax.nn.silu(h @ w_gate) * (h @ w_up)) @ w_down


def setup_inputs(seed: int = 0) -> dict:
    key = jax.random.key(seed)
    ks = iter(jax.random.split(key, 40))
    n_pages = PAST_LEN // PAGE_SIZE
    n_used = DEC_BATCH * n_pages
    n_pool = n_used + max(1, n_used // 4)

    def nrm(shape, scale):
        return jax.random.normal(next(ks), shape, jnp.float32) * scale

    def gain(shape):
        return 1.0 + nrm(shape, 0.05)

    x_prompt = nrm((BATCH, SEQ, D_MODEL), 1.0)
    x_sample = nrm((DEC_BATCH, DEC_SEQ, D_MODEL), 1.0)
    state_gla = nrm((N_EVEN, DEC_BATCH, GLA_HEADS, GLA_DK, GLA_DV), 1.0)
    cache_mla_ckv = nrm((N_EVEN, n_pool, PAGE_SIZE, MLA_KV_RANK), 1.0)
    cache_mla_krope = nrm((N_EVEN, n_pool, PAGE_SIZE, MLA_ROPE_DIM), 1.0)
    cache_diff_k = nrm((N_ODD, n_pool, PAGE_SIZE, DIFF_KV_HEADS, 2 * DIFF_DH), 1.0)
    cache_diff_v = nrm((N_ODD, n_pool, PAGE_SIZE, DIFF_KV_HEADS, DIFF_DV), 1.0)
    page_table = jax.random.permutation(next(ks), n_pool)[:n_used].reshape(DEC_BATCH, n_pages).astype(jnp.int32)
    return {
        'x_prompt': x_prompt,
        'x_sample': x_sample,
        'state_gla': state_gla,
        'cache_mla_ckv': cache_mla_ckv,
        'cache_mla_krope': cache_mla_krope,
        'cache_diff_k': cache_diff_k,
        'cache_diff_v': cache_diff_v,
        'page_table': page_table,
        'norm_mix_pre': gain((DEPTH, D_MODEL)),
        'norm_mix_post': gain((DEPTH, D_MODEL)),
        'norm_ffn_pre': gain((DEPTH, D_MODEL)),
        'norm_ffn_post': gain((DEPTH, D_MODEL)),
        'w_in_even': nrm((N_EVEN, D_MODEL, EVEN_IN), D_MODEL ** -0.5),
        'w_gla_alpha': nrm((N_EVEN, GLA_GATE_RANK, GLA_QK_W), GLA_GATE_RANK ** -0.5),
        'b_gla_alpha': nrm((N_EVEN, GLA_QK_W), 0.01),
        'gla_out_norm': gain((N_EVEN, GLA_DV)),
        'mla_q_norm': gain((N_EVEN, MLA_Q_RANK)),
        'w_mla_uq': nrm((N_EVEN, MLA_Q_RANK, MLA_HEADS * (MLA_NOPE_DIM + MLA_ROPE_DIM)), MLA_Q_RANK ** -0.5),
        'mla_kv_norm': gain((N_EVEN, MLA_KV_RANK)),
        'w_mla_uk': nrm((N_EVEN, MLA_KV_RANK, MLA_HEADS, MLA_NOPE_DIM), MLA_KV_RANK ** -0.5),
        'w_mla_uv': nrm((N_EVEN, MLA_KV_RANK, MLA_HEADS, MLA_V_DIM), MLA_KV_RANK ** -0.5),
        'w_out_even': nrm((N_EVEN, EVEN_MIX, D_MODEL), EVEN_MIX ** -0.5),
        'w_in_odd': nrm((N_ODD, D_MODEL, ODD_IN), D_MODEL ** -0.5),
        'diff_lambda_q1': nrm((N_ODD, DIFF_DH), 0.1),
        'diff_lambda_k1': nrm((N_ODD, DIFF_DH), 0.1),
        'diff_lambda_q2': nrm((N_ODD, DIFF_DH), 0.1),
        'diff_lambda_k2': nrm((N_ODD, DIFF_DH), 0.1),
        'diff_subln': gain((N_ODD, DIFF_DV)),
        'w_out_odd': nrm((N_ODD, ODD_MIX, D_MODEL), ODD_MIX ** -0.5),
        'w_ffn_gate': nrm((DEPTH, D_MODEL, D_FF), D_MODEL ** -0.5),
        'w_ffn_up': nrm((DEPTH, D_MODEL, D_FF), D_MODEL ** -0.5),
        'w_ffn_down': nrm((DEPTH, D_FF, D_MODEL), D_FF ** -0.5),
    }


def reference(x_prompt, x_sample, state_gla, cache_mla_ckv, cache_mla_krope, cache_diff_k, cache_diff_v, page_table,
              norm_mix_pre, norm_mix_post, norm_ffn_pre, norm_ffn_post,
              w_in_even, w_gla_alpha, b_gla_alpha, gla_out_norm, mla_q_norm, w_mla_uq, mla_kv_norm, w_mla_uk, w_mla_uv, w_out_even,
              w_in_odd, diff_lambda_q1, diff_lambda_k1, diff_lambda_q2, diff_lambda_k2, diff_subln, w_out_odd,
              w_ffn_gate, w_ffn_up, w_ffn_down):
    pos_p = jnp.arange(x_prompt.shape[1], dtype=jnp.int32)
    pos_s = PAST_LEN + jnp.arange(x_sample.shape[1], dtype=jnp.int32)
    yp, ys = x_prompt, x_sample
    gla_p, gla_s, ckv_p, kr_p, ckv_s, kr_s = [], [], [], [], [], []
    dk_p, dv_p, dk_s, dv_s = [], [], [], []
    for i in range(DEPTH):
        j = i // 2
        hp = rms_norm(yp, norm_mix_pre[i])
        hs = rms_norm(ys, norm_mix_pre[i])
        if i % 2 == 0:
            ew = (w_in_even[j], w_gla_alpha[j], b_gla_alpha[j], gla_out_norm[j], mla_q_norm[j], w_mla_uq[j],
                  mla_kv_norm[j], w_mla_uk[j], w_mla_uv[j], w_out_even[j])
            s0 = jnp.zeros((x_prompt.shape[0], GLA_HEADS, GLA_DK, GLA_DV), jnp.float32)
            mp, sp, cp, rp = even_mixer(hp, pos_p, s0, None, *ew)
            past = (gather_pages(cache_mla_ckv[j], page_table), gather_pages(cache_mla_krope[j], page_table))
            ms, ss, cs, rs = even_mixer(hs, pos_s, state_gla[j], past, *ew)
            gla_p.append(sp)
            gla_s.append(ss)
            ckv_p.append(cp)
            kr_p.append(rp)
            ckv_s.append(cs)
            kr_s.append(rs)
        else:
            ow = (w_in_odd[j], diff_lambda_q1[j], diff_lambda_k1[j], diff_lambda_q2[j], diff_lambda_k2[j],
                  diff_subln[j], w_out_odd[j])
            mp, kp, vp = odd_mixer(hp, pos_p, None, i, *ow)
            past = (gather_pages(cache_diff_k[j], page_table), gather_pages(cache_diff_v[j], page_table))
            ms, kss, vss = odd_mixer(hs, pos_s, past, i, *ow)
            dk_p.append(kp)
            dv_p.append(vp)
            dk_s.append(kss)
            dv_s.append(vss)
        yp = yp + rms_norm(mp, norm_mix_post[i])
        ys = ys + rms_norm(ms, norm_mix_post[i])
        hp = rms_norm(yp, norm_ffn_pre[i])
        hs = rms_norm(ys, norm_ffn_pre[i])
        yp = yp + rms_norm(swiglu(hp, w_ffn_gate[i], w_ffn_up[i], w_ffn_down[i]), norm_ffn_post[i])
        ys = ys + rms_norm(swiglu(hs, w_ffn_gate[i], w_ffn_up[i], w_ffn_down[i]), norm_ffn_post[i])
    return (yp, ys,
            jnp.stack(gla_p), jnp.stack(gla_s),
            jnp.stack(ckv_p), jnp.stack(kr_p), jnp.stack(ckv_s), jnp.stack(kr_s),
            jnp.stack(dk_p), jnp.stack(dv_p), jnp.stack(dk_s), jnp.stack(dv_s))
```

```python
import functools
import math

import jax
import jax.numpy as jnp
import numpy as np
from jax import lax
from jax.experimental import pallas as pl
from jax.experimental.pallas import tpu as pltpu

F32 = jnp.float32
BF16 = jnp.bfloat16

D_MODEL = 1024
EPS = 1e-6
ROPE_THETA = 10000.0
LANES = 128

GLA_HEADS = 4
GLA_DK = 64
GLA_DV = 128
GLA_QK_W = GLA_HEADS * GLA_DK
GLA_V_W = GLA_HEADS * GLA_DV
GLA_GATE_RANK = 16
GLA_GATE_TEMP = 16.0
GLA_CHUNK = 64
GLA_SUB = 16

MLA_HEADS = 4
MLA_Q_RANK = 384
MLA_KV_RANK = 256
MLA_NOPE_DIM = 128
MLA_ROPE_DIM = 64
MLA_V_DIM = 128
MLA_QK_PAD = MLA_KV_RANK + LANES

DIFF_HEADS = 8
DIFF_KV_HEADS = 2
DIFF_REP = DIFF_HEADS // DIFF_KV_HEADS
DIFF_DH = 64
DIFF_DV = 2 * DIFF_DH
DIFF_Q_W = DIFF_HEADS * 2 * DIFF_DH
DIFF_K_W = DIFF_KV_HEADS * 2 * DIFF_DH
DIFF_V_W = DIFF_KV_HEADS * DIFF_DV

EVEN_IN_PAD = 2304
ODD_IN = DIFF_Q_W + DIFF_K_W + DIFF_V_W
LOG2E = 1.4426950408889634
NEG_BIG = -1e30

VMEM_LIMIT = 56 * 1024 * 1024


def _cparams(sem):
    return pltpu.CompilerParams(dimension_semantics=sem, vmem_limit_bytes=VMEM_LIMIT)


def _rms(x, g):
    return x * lax.rsqrt(jnp.mean(x * x, axis=-1, keepdims=True) + EPS) * g


def _silu(x):
    return x * (1.0 / (1.0 + jnp.exp(-x)))


def _swap_halves64(x):
    lane = lax.broadcasted_iota(jnp.int32, x.shape, 1)
    return jnp.where((lane & 63) < 32, pltpu.roll(x, 96, axis=1), pltpu.roll(x, 32, axis=1))


def _rope128(x, cos, sin):
    return x * cos + _swap_halves64(x) * sin


def _table_spec(table, tm):
    n_tiles = table.shape[0] // tm
    return pl.BlockSpec((tm, LANES), lambda i: (i % n_tiles, 0))


def _dot(a, b):
    return jnp.dot(a, b, preferred_element_type=F32)


def _dot_nt(a, b):
    return lax.dot_general(a, b, (((1,), (1,)), ((), ())), preferred_element_type=F32)


def _even_proj_kernel(y_ref, cos_ref, sin_ref, gpre_ref, win_ref, walpha_ref, balpha_ref, qnorm_ref, kvnorm_ref,
                      wuq_ref, wukt_ref,
                      qg_ref, kg_ref, vg_ref, rg_ref, la_ref, ckv_ref, krope_ref, qcat_ref, kcat_ref):
    h = _rms(y_ref[...], gpre_ref[...]).astype(BF16)
    p = _dot(h, win_ref[...])
    qg_ref[...] = p[:, 0:256] * (GLA_DK ** -0.5)
    kg_ref[...] = p[:, 256:512]
    vg_ref[...] = p[:, 512:1024]
    rg_ref[...] = p[:, 1024:1536]
    cos = cos_ref[...]
    sin = sin_ref[...]
    lane = lax.broadcasted_iota(jnp.int32, cos.shape, 1)
    low = lane < MLA_ROPE_DIM

    tail = p[:, 2176:2304]
    z = _dot(tail.astype(BF16), walpha_ref[...]) + balpha_ref[...]
    la_ref[...] = (jnp.minimum(z, 0.0) - jnp.log(1.0 + jnp.exp(-jnp.abs(z)))) * (1.0 / GLA_GATE_TEMP)
    kr = _rope128(tail, cos, sin)
    krope_ref[...] = kr[:, :MLA_ROPE_DIM]

    ckv = _rms(p[:, 1920:2176], kvnorm_ref[...])
    ckv_ref[...] = ckv
    kcat_ref[:, 0:MLA_KV_RANK] = ckv.astype(BF16)
    kcat_ref[:, MLA_KV_RANK:MLA_QK_PAD] = jnp.where(low, kr, 0.0).astype(BF16)

    cq = _rms(p[:, 1536:1920], qnorm_ref[...]).astype(BF16)
    q = _dot(cq, wuq_ref[...])
    sm = (MLA_NOPE_DIM + MLA_ROPE_DIM) ** -0.5 * LOG2E
    for s in range(MLA_HEADS // 2):
        slab = _rope128(q[:, 512 + 128 * s:640 + 128 * s], cos, sin) * sm
        for hh in range(2):
            head = 2 * s + hh
            q_lat = _dot(q[:, 128 * head:128 * head + 128].astype(BF16), wukt_ref[head]) * sm
            qcat_ref[head, :, 0:MLA_KV_RANK] = q_lat.astype(BF16)
            rp = slab if hh == 0 else pltpu.roll(slab, 64, axis=1)
            qcat_ref[head, :, MLA_KV_RANK:MLA_QK_PAD] = jnp.where(low, rp, 0.0).astype(BF16)


def _even_proj(y, cos, sin, w, tm):
    t = y.shape[0]
    grid = (t // tm,)
    row = lambda n: pl.BlockSpec((tm, n), lambda i: (i, 0))
    full2 = lambda a: pl.BlockSpec(a.shape, lambda i: (0, 0))
    full3 = lambda a: pl.BlockSpec(a.shape, lambda i: (0, 0, 0))
    tab = _table_spec(cos, tm)
    outs = [
        jax.ShapeDtypeStruct((t, 256), F32), jax.ShapeDtypeStruct((t, 256), F32),
        jax.ShapeDtypeStruct((t, 512), F32), jax.ShapeDtypeStruct((t, 512), F32),
        jax.ShapeDtypeStruct((t, 256), F32),
        jax.ShapeDtypeStruct((t, MLA_KV_RANK), F32), jax.ShapeDtypeStruct((t, MLA_ROPE_DIM), F32),
        jax.ShapeDtypeStruct((MLA_HEADS, t, MLA_QK_PAD), BF16), jax.ShapeDtypeStruct((t, MLA_QK_PAD), BF16),
    ]
    out_specs = [row(256), row(256), row(512), row(512), row(256), row(MLA_KV_RANK), row(MLA_ROPE_DIM),
                 pl.BlockSpec((MLA_HEADS, tm, MLA_QK_PAD), lambda i: (0, i, 0)), row(MLA_QK_PAD)]
    return pl.pallas_call(
        _even_proj_kernel,
        grid=grid,
        in_specs=[row(D_MODEL), tab, tab, full2(w["g_pre"]), full2(w["w_in"]), full2(w["w_alpha"]),
                  full2(w["b_alpha"]), full2(w["q_norm"]), full2(w["kv_norm"]), full2(w["w_uq"]), full3(w["w_ukt"])],
        out_specs=out_specs,
        out_shape=outs,
        compiler_params=_cparams(("parallel",)),
        name="even_proj",
    )(y, cos, sin, w["g_pre"], w["w_in"], w["w_alpha"], w["b_alpha"], w["q_norm"], w["kv_norm"], w["w_uq"], w["w_ukt"])


def _gla_prompt_kernel(q_ref, k_ref, la_ref, v_ref, o_ref, sfin_ref, s_ref):
    c = pl.program_id(1)
    C = GLA_CHUNK
    hi = lax.Precision.HIGHEST

    @pl.when(c == 0)
    def _():
        s_ref[...] = jnp.zeros_like(s_ref)

    row = lax.broadcasted_iota(jnp.int32, (C, C), 0)
    col = lax.broadcasted_iota(jnp.int32, (C, C), 1)
    tri = (col <= row).astype(F32)
    sub_start = (col < (row & -GLA_SUB)).astype(F32)
    ones = jnp.ones((C, LANES), F32)
    lane = lax.broadcasted_iota(jnp.int32, (C, LANES), 1)
    tloc = lax.broadcasted_iota(jnp.int32, (C, LANES), 0) & (GLA_SUB - 1)
    for slab in range(GLA_HEADS // 2):
        seg = slice(LANES * slab, LANES * (slab + 1))
        la = la_ref[:, seg]
        q = q_ref[:, seg]
        k = k_ref[:, seg]
        cum = jnp.dot(tri, la, preferred_element_type=F32, precision=hi)
        ref = jnp.dot(sub_start, la, preferred_element_type=F32, precision=hi)
        g_last = cum[C - 1:C, :]
        s_old = s_ref[slab]
        s_old_b = s_old.astype(BF16)
        q_state = q * jnp.exp(cum)
        q_off = q * jnp.exp(cum - ref)
        k_dec = k * jnp.exp(g_last - cum)
        diag = [q * k]
        for d in range(1, GLA_SUB):
            diag.append(jnp.where(tloc >= d,
                                  q * pltpu.roll(k, d, axis=0) * jnp.exp(cum - pltpu.roll(cum, d, axis=0)), 0.0))
        dec = jnp.exp(lax.dot_general(la, ones, (((0,), (0,)), ((), ())), preferred_element_type=F32, precision=hi))
        s_new = dec * s_old
        k_off = [(k * jnp.exp(jnp.minimum(ref[GLA_SUB * i:GLA_SUB * i + 1, :] - cum, 0.0))).astype(BF16)
                 for i in range(1, C // GLA_SUB)]
        for hh in range(2):
            head = 2 * slab + hh
            mine = (lane >= GLA_DK * hh) & (lane < GLA_DK * (hh + 1))
            v = v_ref[:, GLA_DV * head:GLA_DV * (head + 1)]
            vb = v.astype(BF16)
            o = _dot(jnp.where(mine, q_state, 0.0).astype(BF16), s_old_b)
            qt = jnp.where(mine, q_off, 0.0).astype(BF16)
            a_off = jnp.zeros((C, C), F32)
            for i in range(1, C // GLA_SUB):
                in_block = (row >= GLA_SUB * i) & (row < GLA_SUB * (i + 1)) & (col < GLA_SUB * i)
                a_off = jnp.where(in_block, _dot_nt(qt, k_off[i - 1]), a_off)
            o = o + _dot(a_off.astype(BF16), vb)
            for d in range(GLA_SUB):
                a_d = jnp.sum(jnp.where(mine, diag[d], 0.0), axis=1, keepdims=True)
                o = o + a_d * (v if d == 0 else pltpu.roll(v, d, axis=0))
            o_ref[:, GLA_DV * head:GLA_DV * (head + 1)] = o
            s_new = s_new + lax.dot_general(jnp.where(mine, k_dec, 0.0).astype(BF16), vb, (((0,), (0,)), ((), ())),
                                            preferred_element_type=F32)
        s_ref[slab] = s_new

    @pl.when(c == pl.num_programs(1) - 1)
    def _():
        sfin_ref[0] = s_ref[...]


def _gla_prompt(qg, kg, la, vg, batch):
    t = qg.shape[0]
    nc = t // batch // GLA_CHUNK
    n_slab = GLA_HEADS // 2
    row = lambda n: pl.BlockSpec((GLA_CHUNK, n), lambda b, c: (b * nc + c, 0))
    o, s_fin = pl.pallas_call(
        _gla_prompt_kernel,
        grid=(batch, nc),
        in_specs=[row(256), row(256), row(256), row(512)],
        out_specs=[row(512), pl.BlockSpec((1, n_slab, 2 * GLA_DK, GLA_DV), lambda b, c: (b, 0, 0, 0))],
        out_shape=[jax.ShapeDtypeStruct((t, GLA_V_W), F32),
                   jax.ShapeDtypeStruct((batch, n_slab, 2 * GLA_DK, GLA_DV), F32)],
        scratch_shapes=[pltpu.VMEM((n_slab, 2 * GLA_DK, GLA_DV), F32)],
        compiler_params=_cparams(("parallel", "arbitrary")),
        name="gla_prompt",
    )(qg, kg, la, vg)
    return o, s_fin.reshape(batch, GLA_HEADS, GLA_DK, GLA_DV)


GLA_DEC_BLOCK = 16


def _gla_sample_kernel(qt_ref, kt_ref, lat_ref, v_ref, s_ref, o_ref, snew_ref):
    for i in range(GLA_DEC_BLOCK):
        for h in range(GLA_HEADS):
            rows = slice(GLA_DK * h, GLA_DK * (h + 1))
            qc = qt_ref[0, rows, i:i + 1]
            kc = kt_ref[0, rows, i:i + 1]
            ac = jnp.exp(lat_ref[0, rows, i:i + 1])
            vr = v_ref[i:i + 1, GLA_DV * h:GLA_DV * (h + 1)]
            s_new = ac * s_ref[i, h] + kc * vr
            snew_ref[i, h] = s_new
            o_ref[i:i + 1, GLA_DV * h:GLA_DV * (h + 1)] = jnp.sum(qc * s_new, axis=0, keepdims=True)


def _gla_sample(qg, kg, la, vg, state):
    n = qg.shape[0]
    nb = n // GLA_DEC_BLOCK
    cols = lambda a: a.reshape(nb, GLA_DEC_BLOCK, GLA_QK_W).transpose(0, 2, 1)
    colspec = pl.BlockSpec((1, GLA_QK_W, GLA_DEC_BLOCK), lambda i: (i, 0, 0))
    sspec = pl.BlockSpec((GLA_DEC_BLOCK, GLA_HEADS, GLA_DK, GLA_DV), lambda i: (i, 0, 0, 0))
    return pl.pallas_call(
        _gla_sample_kernel,
        grid=(nb,),
        in_specs=[colspec, colspec, colspec, pl.BlockSpec((GLA_DEC_BLOCK, GLA_V_W), lambda i: (i, 0)), sspec],
        out_specs=[pl.BlockSpec((GLA_DEC_BLOCK, GLA_V_W), lambda i: (i, 0)), sspec],
        out_shape=[jax.ShapeDtypeStruct((n, GLA_V_W), F32), jax.ShapeDtypeStruct(state.shape, F32)],
        compiler_params=_cparams(("parallel",)),
        name="gla_sample",
    )(cols(qg), cols(kg), cols(la), vg, state)


def _causal_pairs(nq):
    qi = np.concatenate([np.full(i + 1, i, np.int32) for i in range(nq)])
    ki = np.concatenate([np.arange(i + 1, dtype=np.int32) for i in range(nq)])
    return jnp.asarray(qi), jnp.asarray(ki)


def _online_softmax_step(s, v, m_ref, l_ref, acc_ref):
    m_prev = m_ref[...]
    m_new = jnp.maximum(m_prev, jnp.max(s, axis=1, keepdims=True))
    alpha = jnp.exp2(m_prev - m_new)
    p = jnp.exp2(s - m_new)
    l_ref[...] = alpha * l_ref[...] + jnp.sum(p, axis=1, keepdims=True)
    acc_ref[...] = alpha * acc_ref[...] + _dot(p.astype(BF16), v)
    m_ref[...] = m_new


MLA_TQ = 256


def _mla_prompt_kernel(qi_ref, ki_ref, q_ref, k_ref, o_ref, m_ref, l_ref, acc_ref):
    p = pl.program_id(1)
    qi = qi_ref[p]
    ki = ki_ref[p]
    tq = MLA_TQ
    rows = MLA_HEADS * tq

    @pl.when(ki == 0)
    def _():
        m_ref[...] = jnp.full_like(m_ref, NEG_BIG)
        l_ref[...] = jnp.zeros_like(l_ref)
        acc_ref[...] = jnp.zeros_like(acc_ref)

    q = q_ref[...].reshape(rows, MLA_QK_PAD)
    k = k_ref[...]
    s = _dot_nt(q, k)
    v = k[:, :MLA_KV_RANK]

    @pl.when(ki < qi)
    def _():
        _online_softmax_step(s, v, m_ref, l_ref, acc_ref)

    @pl.when(ki == qi)
    def _():
        tok = lax.broadcasted_iota(jnp.int32, (rows, tq), 0) & (tq - 1)
        key = lax.broadcasted_iota(jnp.int32, (rows, tq), 1)
        _online_softmax_step(jnp.where(key <= tok, s, NEG_BIG), v, m_ref, l_ref, acc_ref)
        o_ref[...] = (acc_ref[...] / l_ref[...]).reshape(MLA_HEADS, tq, MLA_KV_RANK)


def _mla_prompt(qcat, kcat, batch):
    t = kcat.shape[0]
    tq = MLA_TQ
    nq = t // batch // tq
    qi, ki = _causal_pairs(nq)
    grid_spec = pltpu.PrefetchScalarGridSpec(
        num_scalar_prefetch=2,
        grid=(batch, int(qi.shape[0])),
        in_specs=[pl.BlockSpec((MLA_HEADS, tq, MLA_QK_PAD), lambda b, p, qi, ki: (0, b * nq + qi[p], 0)),
                  pl.BlockSpec((tq, MLA_QK_PAD), lambda b, p, qi, ki: (b * nq + ki[p], 0))],
        out_specs=pl.BlockSpec((MLA_HEADS, tq, MLA_KV_RANK), lambda b, p, qi, ki: (0, b * nq + qi[p], 0)),
        scratch_shapes=[pltpu.VMEM((MLA_HEADS * tq, 1), F32), pltpu.VMEM((MLA_HEADS * tq, 1), F32),
                        pltpu.VMEM((MLA_HEADS * tq, MLA_KV_RANK), F32)],
    )
    return pl.pallas_call(
        _mla_prompt_kernel,
        grid_spec=grid_spec,
        out_shape=jax.ShapeDtypeStruct((MLA_HEADS, t, MLA_KV_RANK), F32),
        compiler_params=_cparams(("parallel", "arbitrary")),
        name="mla_prompt",
    )(qi, ki, qcat, kcat)


PAGES_PER_STEP = 16
DEC_ROWS = 8


def _mla_sample_kernel(pt_ref, q_ref, knew_ref, *refs):
    pp = PAGES_PER_STEP
    ckv_refs = refs[:pp]
    kr_refs = refs[pp:2 * pp]
    o_ref, m_ref, l_ref, acc_ref = refs[2 * pp:]
    c = pl.program_id(1)

    @pl.when(c == 0)
    def _():
        m_ref[...] = jnp.full_like(m_ref, NEG_BIG)
        l_ref[...] = jnp.zeros_like(l_ref)
        acc_ref[...] = jnp.zeros_like(acc_ref)

    q = q_ref[0]
    q_lat = q[:, :MLA_KV_RANK]
    q_rope = q[:, MLA_KV_RANK:MLA_KV_RANK + MLA_ROPE_DIM]
    pages = [r[...].astype(BF16) for r in ckv_refs]
    s = jnp.concatenate(
        [_dot_nt(q_lat, pages[i]) + _dot_nt(q_rope, kr_refs[i][...].astype(BF16)) for i in range(pp)], axis=1)
    m_prev = m_ref[...]
    m_new = jnp.maximum(m_prev, jnp.max(s, axis=1, keepdims=True))
    alpha = jnp.exp2(m_prev - m_new)
    p = jnp.exp2(s - m_new)
    l_ref[...] = alpha * l_ref[...] + jnp.sum(p, axis=1, keepdims=True)
    pb = p.astype(BF16)
    psz = pages[0].shape[0]
    pv = _dot(pb[:, :psz], pages[0])
    for i in range(1, pp):
        pv = pv + _dot(pb[:, psz * i:psz * (i + 1)], pages[i])
    acc_ref[...] = alpha * acc_ref[...] + pv
    m_ref[...] = m_new

    @pl.when(c == pl.num_programs(1) - 1)
    def _():
        knew = knew_ref[0]
        s_new = jnp.sum(q.astype(F32) * knew.astype(F32), axis=1, keepdims=True)
        m_old = m_ref[...]
        m_fin = jnp.maximum(m_old, s_new)
        a = jnp.exp2(m_old - m_fin)
        p_new = jnp.exp2(s_new - m_fin)
        l_fin = a * l_ref[...] + p_new
        v_new = knew[:, :MLA_KV_RANK].astype(F32)
        o_ref[0] = (a * acc_ref[...] + p_new.astype(BF16).astype(F32) * v_new) / l_fin


def _mla_sample(page_table, q, knew, pool_ckv, pool_kr):
    n, n_pages = page_table.shape
    pp = PAGES_PER_STEP
    page = pool_ckv.shape[1]

    def pool_spec(width, i):
        return pl.BlockSpec((None, page, width), lambda b, c, pt: (pt[b * n_pages + c * pp + i], 0, 0))

    grid_spec = pltpu.PrefetchScalarGridSpec(
        num_scalar_prefetch=1,
        grid=(n, n_pages // pp),
        in_specs=([pl.BlockSpec((1, DEC_ROWS, MLA_QK_PAD), lambda b, c, pt: (b, 0, 0)),
                   pl.BlockSpec((1, 1, MLA_QK_PAD), lambda b, c, pt: (b, 0, 0))]
                  + [pool_spec(MLA_KV_RANK, i) for i in range(pp)]
                  + [pool_spec(MLA_ROPE_DIM, i) for i in range(pp)]),
        out_specs=pl.BlockSpec((1, DEC_ROWS, MLA_KV_RANK), lambda b, c, pt: (b, 0, 0)),
        scratch_shapes=[pltpu.VMEM((DEC_ROWS, 1), F32), pltpu.VMEM((DEC_ROWS, 1), F32),
                        pltpu.VMEM((DEC_ROWS, MLA_KV_RANK), F32)],
    )
    return pl.pallas_call(
        _mla_sample_kernel,
        grid_spec=grid_spec,
        out_shape=jax.ShapeDtypeStruct((n, DEC_ROWS, MLA_KV_RANK), F32),
        compiler_params=_cparams(("parallel", "arbitrary")),
        name="mla_sample",
    )(page_table.reshape(-1), q, knew, *([pool_ckv] * pp), *([pool_kr] * pp))


def _even_mix_kernel(y_ref, og_ref, rg_ref, olat_ref, glanorm_ref, wuv_ref, wout_ref, gpost_ref, out_ref):
    pieces = []
    for h in range(GLA_HEADS):
        seg = slice(GLA_DV * h, GLA_DV * (h + 1))
        pieces.append((_rms(og_ref[:, seg], glanorm_ref[...]) * _silu(rg_ref[:, seg])).astype(BF16))
    for h in range(MLA_HEADS):
        pieces.append(_dot(olat_ref[h].astype(BF16), wuv_ref[h]).astype(BF16))
    mix = jnp.concatenate(pieces, axis=1)
    out_ref[...] = y_ref[...] + _rms(_dot(mix, wout_ref[...]), gpost_ref[...])


def _even_mix(y, og, rg, olat, w, tm):
    t = y.shape[0]
    row = lambda n: pl.BlockSpec((tm, n), lambda i: (i, 0))
    full2 = lambda a: pl.BlockSpec(a.shape, lambda i: (0, 0))
    full3 = lambda a: pl.BlockSpec(a.shape, lambda i: (0, 0, 0))
    return pl.pallas_call(
        _even_mix_kernel,
        grid=(t // tm,),
        in_specs=[row(D_MODEL), row(GLA_V_W), row(GLA_V_W),
                  pl.BlockSpec((MLA_HEADS, tm, MLA_KV_RANK), lambda i: (0, i, 0)),
                  full2(w["gla_norm"]), full3(w["w_uv"]), full2(w["w_out"]), full2(w["g_post"])],
        out_specs=row(D_MODEL),
        out_shape=jax.ShapeDtypeStruct((t, D_MODEL), F32),
        compiler_params=_cparams(("parallel",)),
        name="even_mix",
    )(y, og, rg, olat, w["gla_norm"], w["w_uv"], w["w_out"], w["g_post"])


def _odd_proj_kernel(y_ref, cos_ref, sin_ref, gpre_ref, win_ref, dk_ref, dv_ref, qd_ref, kd_ref, vd_ref):
    h = _rms(y_ref[...], gpre_ref[...]).astype(BF16)
    p = _dot(h, win_ref[...])
    cos = cos_ref[...]
    sin = sin_ref[...]
    lane = lax.broadcasted_iota(jnp.int32, cos.shape, 1)
    low = lane < DIFF_DH
    sm = DIFF_DH ** -0.5 * LOG2E
    for head in range(DIFF_HEADS):
        g, r = divmod(head, DIFF_REP)
        slab = _rope128(p[:, LANES * head:LANES * (head + 1)], cos, sin) * sm
        qd_ref[g, r] = jnp.where(low, slab, 0.0).astype(BF16)
        qd_ref[g, DIFF_REP + r] = jnp.where(low, 0.0, slab).astype(BF16)
    for g in range(DIFF_KV_HEADS):
        seg = slice(LANES * g, LANES * (g + 1))
        kk = _rope128(p[:, DIFF_Q_W + LANES * g:DIFF_Q_W + LANES * (g + 1)], cos, sin)
        dk_ref[:, seg] = kk
        kd_ref[:, seg] = kk.astype(BF16)
    vv = p[:, DIFF_Q_W + DIFF_K_W:ODD_IN]
    dv_ref[...] = vv
    vd_ref[...] = vv.astype(BF16)


def _odd_proj(y, cos, sin, w, tm):
    t = y.shape[0]
    row = lambda n: pl.BlockSpec((tm, n), lambda i: (i, 0))
    full2 = lambda a: pl.BlockSpec(a.shape, lambda i: (0, 0))
    return pl.pallas_call(
        _odd_proj_kernel,
        grid=(t // tm,),
        in_specs=[row(D_MODEL), _table_spec(cos, tm), _table_spec(cos, tm), full2(w["g_pre"]), full2(w["w_in"])],
        out_specs=[row(DIFF_K_W), row(DIFF_V_W),
                   pl.BlockSpec((DIFF_KV_HEADS, 2 * DIFF_REP, tm, LANES), lambda i: (0, 0, i, 0)),
                   row(DIFF_K_W), row(DIFF_V_W)],
        out_shape=[jax.ShapeDtypeStruct((t, DIFF_K_W), F32), jax.ShapeDtypeStruct((t, DIFF_V_W), F32),
                   jax.ShapeDtypeStruct((DIFF_KV_HEADS, 2 * DIFF_REP, t, LANES), BF16),
                   jax.ShapeDtypeStruct((t, DIFF_K_W), BF16), jax.ShapeDtypeStruct((t, DIFF_V_W), BF16)],
        compiler_params=_cparams(("parallel",)),
        name="odd_proj",
    )(y, cos, sin, w["g_pre"], w["w_in"])


def _diff_lambda(lam_ref, lam_init):
    lq1, lk1, lq2, lk2 = lam_ref[0:1, :], lam_ref[1:2, :], lam_ref[2:3, :], lam_ref[3:4, :]
    return (jnp.exp(jnp.sum(lq1 * lk1, axis=1, keepdims=True))
            - jnp.exp(jnp.sum(lq2 * lk2, axis=1, keepdims=True)) + lam_init)


DIFF_TQ = 128


def _diff_prompt_kernel(qi_ref, ki_ref, lam_ref, q_ref, k_ref, v_ref, o_ref, m_ref, l_ref, acc_ref, *, lam_init):
    p = pl.program_id(2)
    qi = qi_ref[p]
    ki = ki_ref[p]
    tq = DIFF_TQ
    rows = 2 * DIFF_REP * tq

    @pl.when(ki == 0)
    def _():
        m_ref[...] = jnp.full_like(m_ref, NEG_BIG)
        l_ref[...] = jnp.zeros_like(l_ref)
        acc_ref[...] = jnp.zeros_like(acc_ref)

    q = q_ref[0].reshape(rows, LANES)
    s = _dot_nt(q, k_ref[...])
    v = v_ref[...]

    @pl.when(ki < qi)
    def _():
        _online_softmax_step(s, v, m_ref, l_ref, acc_ref)

    @pl.when(ki == qi)
    def _():
        tok = lax.broadcasted_iota(jnp.int32, (rows, tq), 0) & (tq - 1)
        key = lax.broadcasted_iota(jnp.int32, (rows, tq), 1)
        _online_softmax_step(jnp.where(key <= tok, s, NEG_BIG), v, m_ref, l_ref, acc_ref)
        lam = _diff_lambda(lam_ref, lam_init)
        o = acc_ref[...] / l_ref[...]
        half = DIFF_REP * tq
        for r in range(DIFF_REP):
            o_ref[:, DIFF_DV * r:DIFF_DV * (r + 1)] = (
                o[tq * r:tq * (r + 1)] - lam * o[half + tq * r:half + tq * (r + 1)])


def _diff_prompt(qd, kd, vd, lam_vecs, lam_init, batch):
    t = kd.shape[0]
    tq = DIFF_TQ
    nq = t // batch // tq
    qi, ki = _causal_pairs(nq)
    rows = 2 * DIFF_REP * tq
    grid_spec = pltpu.PrefetchScalarGridSpec(
        num_scalar_prefetch=2,
        grid=(batch, DIFF_KV_HEADS, int(qi.shape[0])),
        in_specs=[pl.BlockSpec(lam_vecs.shape, lambda b, g, p, qi, ki: (0, 0)),
                  pl.BlockSpec((1, 2 * DIFF_REP, tq, LANES), lambda b, g, p, qi, ki: (g, 0, b * nq + qi[p], 0)),
                  pl.BlockSpec((tq, LANES), lambda b, g, p, qi, ki: (b * nq + ki[p], g)),
                  pl.BlockSpec((tq, DIFF_DV), lambda b, g, p, qi, ki: (b * nq + ki[p], g))],
        out_specs=pl.BlockSpec((tq, DIFF_REP * DIFF_DV), lambda b, g, p, qi, ki: (b * nq + qi[p], g)),
        scratch_shapes=[pltpu.VMEM((rows, 1), F32), pltpu.VMEM((rows, 1), F32), pltpu.VMEM((rows, DIFF_DV), F32)],
    )
    return pl.pallas_call(
        functools.partial(_diff_prompt_kernel, lam_init=lam_init),
        grid_spec=grid_spec,
        out_shape=jax.ShapeDtypeStruct((t, DIFF_HEADS * DIFF_DV), F32),
        compiler_params=_cparams(("parallel", "parallel", "arbitrary")),
        name="diff_prompt",
    )(qi, ki, lam_vecs, qd, kd, vd)


DIFF_DEC_ROWS = 2 * DIFF_HEADS


def _diff_sample_kernel(pt_ref, lam_ref, q_ref, knew_ref, vnew_ref, *refs, lam_init):
    pp = PAGES_PER_STEP
    k_refs = refs[:pp]
    v_refs = refs[pp:2 * pp]
    o_ref, m_ref, l_ref, acc_ref = refs[2 * pp:]
    c = pl.program_id(1)

    @pl.when(c == 0)
    def _():
        m_ref[...] = jnp.full_like(m_ref, NEG_BIG)
        l_ref[...] = jnp.zeros_like(l_ref)
        acc_ref[...] = jnp.zeros_like(acc_ref)

    q = q_ref[0]
    s = jnp.concatenate([_dot_nt(q, k_refs[i][...].astype(BF16)) for i in range(pp)], axis=1)
    m_prev = m_ref[...]
    m_new = jnp.maximum(m_prev, jnp.max(s, axis=1, keepdims=True))
    alpha = jnp.exp2(m_prev - m_new)
    p = jnp.exp2(s - m_new)
    l_ref[...] = alpha * l_ref[...] + jnp.sum(p, axis=1, keepdims=True)
    pb = p.astype(BF16)
    psz = k_refs[0].shape[0]
    pv = _dot(pb[:, :psz], v_refs[0][...].astype(BF16))
    for i in range(1, pp):
        pv = pv + _dot(pb[:, psz * i:psz * (i + 1)], v_refs[i][...].astype(BF16))
    acc_ref[...] = alpha * acc_ref[...] + pv
    m_ref[...] = m_new

    @pl.when(c == pl.num_programs(1) - 1)
    def _():
        knew = knew_ref[0].astype(F32)
        s_new = jnp.sum(q.astype(F32) * knew, axis=1, keepdims=True)
        m_old = m_ref[...]
        m_fin = jnp.maximum(m_old, s_new)
        a = jnp.exp2(m_old - m_fin)
        p_new = jnp.exp2(s_new - m_fin)
        l_fin = a * l_ref[...] + p_new
        v_new = vnew_ref[0].astype(F32)
        o = (a * acc_ref[...] + p_new.astype(BF16).astype(F32) * v_new) / l_fin
        lam = _diff_lambda(lam_ref, lam_init)
        for g in range(DIFF_KV_HEADS):
            base = 2 * DIFF_REP * g
            o0 = o[base:base + DIFF_REP, DIFF_DV * g:DIFF_DV * (g + 1)]
            o1 = o[base + DIFF_REP:base + 2 * DIFF_REP, DIFF_DV * g:DIFF_DV * (g + 1)]
            o_ref[0, DIFF_REP * g:DIFF_REP * (g + 1), :] = o0 - lam * o1


def _diff_sample(page_table, lam_vecs, lam_init, q, knew, vnew, pool_k, pool_v):
    n, n_pages = page_table.shape
    pp = PAGES_PER_STEP
    page = pool_k.shape[1]
    width = pool_k.shape[2]

    def pool_spec(i):
        return pl.BlockSpec((None, page, width), lambda b, c, pt: (pt[b * n_pages + c * pp + i], 0, 0))

    grid_spec = pltpu.PrefetchScalarGridSpec(
        num_scalar_prefetch=1,
        grid=(n, n_pages // pp),
        in_specs=([pl.BlockSpec(lam_vecs.shape, lambda b, c, pt: (0, 0)),
                   pl.BlockSpec((1, DIFF_DEC_ROWS, width), lambda b, c, pt: (b, 0, 0)),
                   pl.BlockSpec((1, 1, width), lambda b, c, pt: (b, 0, 0)),
                   pl.BlockSpec((1, 1, width), lambda b, c, pt: (b, 0, 0))]
                  + [pool_spec(i) for i in range(pp)] + [pool_spec(i) for i in range(pp)]),
        out_specs=pl.BlockSpec((1, DIFF_HEADS, DIFF_DV), lambda b, c, pt: (b, 0, 0)),
        scratch_shapes=[pltpu.VMEM((DIFF_DEC_ROWS, 1), F32), pltpu.VMEM((DIFF_DEC_ROWS, 1), F32),
                        pltpu.VMEM((DIFF_DEC_ROWS, width), F32)],
    )
    return pl.pallas_call(
        functools.partial(_diff_sample_kernel, lam_init=lam_init),
        grid_spec=grid_spec,
        out_shape=jax.ShapeDtypeStruct((n, DIFF_HEADS, DIFF_DV), F32),
        compiler_params=_cparams(("parallel", "arbitrary")),
        name="diff_sample",
    )(page_table.reshape(-1), lam_vecs, q, knew, vnew, *([pool_k] * pp), *([pool_v] * pp))


def _odd_mix_kernel(y_ref, o_ref, subln_ref, wout_ref, gpost_ref, out_ref, *, lam_init):
    pieces = []
    for h in range(DIFF_HEADS):
        seg = slice(DIFF_DV * h, DIFF_DV * (h + 1))
        pieces.append((_rms(o_ref[:, seg], subln_ref[...]) * (1.0 - lam_init)).astype(BF16))
    mix = jnp.concatenate(pieces, axis=1)
    out_ref[...] = y_ref[...] + _rms(_dot(mix, wout_ref[...]), gpost_ref[...])


def _odd_mix(y, o, w, lam_init, tm):
    t = y.shape[0]
    row = lambda n: pl.BlockSpec((tm, n), lambda i: (i, 0))
    full2 = lambda a: pl.BlockSpec(a.shape, lambda i: (0, 0))
    return pl.pallas_call(
        functools.partial(_odd_mix_kernel, lam_init=lam_init),
        grid=(t // tm,),
        in_specs=[row(D_MODEL), row(D_MODEL), full2(w["subln"]), full2(w["w_out"]), full2(w["g_post"])],
        out_specs=row(D_MODEL),
        out_shape=jax.ShapeDtypeStruct((t, D_MODEL), F32),
        compiler_params=_cparams(("parallel",)),
        name="odd_mix",
    )(y, o, w["subln"], w["w_out"], w["g_post"])


def _ffn_kernel(y_ref, gpre_ref, wg_ref, wu_ref, wd_ref, gpost_ref, out_ref, h_ref, acc_ref):
    f = pl.program_id(1)

    @pl.when(f == 0)
    def _():
        h_ref[...] = _rms(y_ref[...], gpre_ref[...]).astype(BF16)
        acc_ref[...] = jnp.zeros_like(acc_ref)

    h = h_ref[...]
    a = (_silu(_dot(h, wg_ref[...])) * _dot(h, wu_ref[...])).astype(BF16)
    acc_ref[...] += _dot(a, wd_ref[...])

    @pl.when(f == pl.num_programs(1) - 1)
    def _():
        out_ref[...] = y_ref[...] + _rms(acc_ref[...], gpost_ref[...])


def _ffn(y, w, tm):
    t = y.shape[0]
    d_ff = w["w_gate"].shape[1]
    tf = d_ff // 2 if (d_ff // 2) % LANES == 0 else d_ff
    return pl.pallas_call(
        _ffn_kernel,
        grid=(t // tm, d_ff // tf),
        in_specs=[pl.BlockSpec((tm, D_MODEL), lambda i, f: (i, 0)),
                  pl.BlockSpec((1, D_MODEL), lambda i, f: (0, 0)),
                  pl.BlockSpec((D_MODEL, tf), lambda i, f: (0, f)),
                  pl.BlockSpec((D_MODEL, tf), lambda i, f: (0, f)),
                  pl.BlockSpec((tf, D_MODEL), lambda i, f: (f, 0)),
                  pl.BlockSpec((1, D_MODEL), lambda i, f: (0, 0))],
        out_specs=pl.BlockSpec((tm, D_MODEL), lambda i, f: (i, 0)),
        out_shape=jax.ShapeDtypeStruct((t, D_MODEL), F32),
        scratch_shapes=[pltpu.VMEM((tm, D_MODEL), BF16), pltpu.VMEM((tm, D_MODEL), F32)],
        compiler_params=_cparams(("parallel", "arbitrary")),
        name="ffn",
    )(y, w["g_pre"], w["w_gate"], w["w_up"], w["w_down"], w["g_post"])


def _rope_tables(pos):
    half = MLA_ROPE_DIM // 2
    inv = ROPE_THETA ** (-jnp.arange(0, MLA_ROPE_DIM, 2, dtype=F32) / MLA_ROPE_DIM)
    ang = pos.astype(F32)[:, None] * inv[None, :]
    cos, sin = jnp.cos(ang), jnp.sin(ang)
    assert cos.shape[1] == half
    return jnp.tile(cos, (1, 4)), jnp.tile(jnp.concatenate([-sin, sin], axis=1), (1, 2))


def _row(v):
    return v.reshape(1, -1).astype(F32)


def _pack_even(j, norm_mix_pre_i, norm_mix_post_i, w_in_even, w_gla_alpha, b_gla_alpha, gla_out_norm, mla_q_norm,
               w_mla_uq, mla_kv_norm, w_mla_uk, w_mla_uv, w_out_even):
    w_in = w_in_even[j]
    widths = [GLA_QK_W, GLA_QK_W, GLA_V_W, GLA_GATE_RANK, GLA_V_W, MLA_Q_RANK, MLA_KV_RANK, MLA_ROPE_DIM]
    offs = np.concatenate([[0], np.cumsum(widths)])
    seg = lambda n: w_in[:, int(offs[n]):int(offs[n + 1])]
    pad = jnp.zeros((D_MODEL, EVEN_IN_PAD - int(offs[-1])), w_in.dtype)
    packed = jnp.concatenate([seg(0), seg(1), seg(2), seg(4), seg(5), seg(6), seg(7), seg(3), pad], axis=1)
    w_alpha = jnp.zeros((LANES, GLA_QK_W), F32).at[MLA_ROPE_DIM:MLA_ROPE_DIM + GLA_GATE_RANK].set(w_gla_alpha[j])
    uq = w_mla_uq[j].reshape(MLA_Q_RANK, MLA_HEADS, MLA_NOPE_DIM + MLA_ROPE_DIM)
    uq = jnp.concatenate([uq[:, :, :MLA_NOPE_DIM].reshape(MLA_Q_RANK, -1),
                          uq[:, :, MLA_NOPE_DIM:].reshape(MLA_Q_RANK, -1)], axis=1)
    return {
        "g_pre": _row(norm_mix_pre_i), "g_post": _row(norm_mix_post_i),
        "w_in": packed.astype(BF16), "w_alpha": w_alpha.astype(BF16), "b_alpha": _row(b_gla_alpha[j]),
        "q_norm": _row(mla_q_norm[j]), "kv_norm": _row(mla_kv_norm[j]), "w_uq": uq.astype(BF16),
        "w_ukt": jnp.transpose(w_mla_uk[j], (1, 2, 0)).astype(BF16),
        "w_uv": jnp.transpose(w_mla_uv[j], (1, 0, 2)).astype(BF16),
        "gla_norm": _row(gla_out_norm[j]), "w_out": w_out_even[j].astype(BF16),
    }


def _decode_q_mla(qcat):
    q = jnp.transpose(qcat, (1, 0, 2))
    return jnp.pad(q, ((0, 0), (0, DEC_ROWS - MLA_HEADS), (0, 0)))


def _decode_q_diff(qd):
    n = qd.shape[2]
    q = jnp.transpose(qd, (2, 0, 1, 3))
    z = jnp.zeros_like(q[:, 0])
    return jnp.concatenate([jnp.concatenate([q[:, 0], z], axis=-1), jnp.concatenate([z, q[:, 1]], axis=-1)], axis=1)


def kernel(x_prompt, x_sample, state_gla, cache_mla_ckv, cache_mla_krope, cache_diff_k, cache_diff_v, page_table,
           norm_mix_pre, norm_mix_post, norm_ffn_pre, norm_ffn_post,
           w_in_even, w_gla_alpha, b_gla_alpha, gla_out_norm, mla_q_norm, w_mla_uq, mla_kv_norm, w_mla_uk, w_mla_uv,
           w_out_even,
           w_in_odd, diff_lambda_q1, diff_lambda_k1, diff_lambda_q2, diff_lambda_k2, diff_subln, w_out_odd,
           w_ffn_gate, w_ffn_up, w_ffn_down):
    batch, seq, _ = x_prompt.shape
    n_dec, dec_seq, _ = x_sample.shape
    assert dec_seq == 1
    depth = norm_mix_pre.shape[0]
    n_pages = page_table.shape[1]
    page = cache_mla_ckv.shape[2]
    past_len = n_pages * page
    tp = batch * seq
    tm_p = min(512, seq)
    tm_s = n_dec

    cos_p, sin_p = _rope_tables(jnp.arange(seq, dtype=jnp.int32))
    cos_s, sin_s = _rope_tables(jnp.full((n_dec,), past_len, jnp.int32))

    yp = x_prompt.reshape(tp, D_MODEL)
    ys = x_sample.reshape(n_dec, D_MODEL)
    gla_p, gla_s, ckv_p, kr_p, ckv_s, kr_s = [], [], [], [], [], []
    dk_p, dv_p, dk_s, dv_s = [], [], [], []
    for i in range(depth):
        j = i // 2
        if i % 2 == 0:
            w = _pack_even(j, norm_mix_pre[i], norm_mix_post[i], w_in_even, w_gla_alpha, b_gla_alpha, gla_out_norm,
                           mla_q_norm, w_mla_uq, mla_kv_norm, w_mla_uk, w_mla_uv, w_out_even)
            qg, kg, vg, rg, la, ckv, krope, qcat, kcat = _even_proj(yp, cos_p, sin_p, w, tm_p)
            og, s_fin = _gla_prompt(qg, kg, la, vg, batch)
            olat = _mla_prompt(qcat, kcat, batch)
            yp = _even_mix(yp, og, rg, olat, w, tm_p)
            gla_p.append(s_fin)
            ckv_p.append(ckv.reshape(batch, seq, MLA_KV_RANK))
            kr_p.append(krope.reshape(batch, seq, MLA_ROPE_DIM))

            qg, kg, vg, rg, la, ckv, krope, qcat, kcat = _even_proj(ys, cos_s, sin_s, w, tm_s)
            og, s_new = _gla_sample(qg, kg, la, vg, state_gla[j])
            olat = _mla_sample(page_table, _decode_q_mla(qcat), kcat.reshape(n_dec, 1, MLA_QK_PAD),
                               cache_mla_ckv[j], cache_mla_krope[j])
            olat = jnp.transpose(olat[:, :MLA_HEADS], (1, 0, 2))
            ys = _even_mix(ys, og, rg, olat, w, tm_s)
            gla_s.append(s_new)
            ckv_s.append(ckv.reshape(n_dec, 1, MLA_KV_RANK))
            kr_s.append(krope.reshape(n_dec, 1, MLA_ROPE_DIM))
        else:
            lam_init = 0.8 - 0.6 * math.exp(-0.3 * i)
            w = {"g_pre": _row(norm_mix_pre[i]), "g_post": _row(norm_mix_post[i]), "w_in": w_in_odd[j].astype(BF16),
                 "subln": _row(diff_subln[j]), "w_out": w_out_odd[j].astype(BF16)}
            lam_vecs = jnp.stack([diff_lambda_q1[j], diff_lambda_k1[j], diff_lambda_q2[j], diff_lambda_k2[j]]).astype(F32)
            dk, dv, qd, kd, vd = _odd_proj(yp, cos_p, sin_p, w, tm_p)
            o = _diff_prompt(qd, kd, vd, lam_vecs, lam_init, batch)
            yp = _odd_mix(yp, o, w, lam_init, tm_p)
            dk_p.append(dk.reshape(batch, seq, DIFF_KV_HEADS, 2 * DIFF_DH))
            dv_p.append(dv.reshape(batch, seq, DIFF_KV_HEADS, DIFF_DV))

            dk, dv, qd, kd, vd = _odd_proj(ys, cos_s, sin_s, w, tm_s)
            pool_k = cache_diff_k[j].reshape(cache_diff_k.shape[1], page, DIFF_K_W)
            pool_v = cache_diff_v[j].reshape(cache_diff_v.shape[1], page, DIFF_V_W)
            o = _diff_sample(page_table, lam_vecs, lam_init, _decode_q_diff(qd), kd.reshape(n_dec, 1, DIFF_K_W),
                             vd.reshape(n_dec, 1, DIFF_V_W), pool_k, pool_v)
            ys = _odd_mix(ys, o.reshape(n_dec, DIFF_HEADS * DIFF_DV), w, lam_init, tm_s)
            dk_s.append(dk.reshape(n_dec, 1, DIFF_KV_HEADS, 2 * DIFF_DH))
            dv_s.append(dv.reshape(n_dec, 1, DIFF_KV_HEADS, DIFF_DV))
        wf = {"g_pre": _row(norm_ffn_pre[i]), "g_post": _row(norm_ffn_post[i]), "w_gate": w_ffn_gate[i].astype(BF16),
              "w_up": w_ffn_up[i].astype(BF16), "w_down": w_ffn_down[i].astype(BF16)}
        yp = _ffn(yp, wf, tm_p)
        ys = _ffn(ys, wf, tm_s)
    return (yp.reshape(batch, seq, D_MODEL), ys.reshape(n_dec, 1, D_MODEL),
            jnp.stack(gla_p), jnp.stack(gla_s),
            jnp.stack(ckv_p), jnp.stack(kr_p), jnp.stack(ckv_s), jnp.stack(kr_s),
            jnp.stack(dk_p), jnp.stack(dv_p), jnp.stack(dk_s), jnp.stack(dv_s))
```

```python
import functools
import math

import jax
import jax.numpy as jnp
import numpy as np
from jax import lax
from jax.experimental import pallas as pl
from jax.experimental.pallas import tpu as pltpu

F32 = jnp.float32
BF16 = jnp.bfloat16

D_MODEL = 1024
EPS = 1e-6
ROPE_THETA = 10000.0
LANES = 128

GLA_HEADS = 4
GLA_DK = 64
GLA_DV = 128
GLA_QK_W = GLA_HEADS * GLA_DK
GLA_V_W = GLA_HEADS * GLA_DV
GLA_GATE_RANK = 16
GLA_GATE_TEMP = 16.0
GLA_CHUNK = 64
GLA_SUB = 16

MLA_HEADS = 4
MLA_Q_RANK = 384
MLA_KV_RANK = 256
MLA_NOPE_DIM = 128
MLA_ROPE_DIM = 64
MLA_V_DIM = 128
MLA_QK_PAD = MLA_KV_RANK + LANES

DIFF_HEADS = 8
DIFF_KV_HEADS = 2
DIFF_REP = DIFF_HEADS // DIFF_KV_HEADS
DIFF_DH = 64
DIFF_DV = 2 * DIFF_DH
DIFF_Q_W = DIFF_HEADS * 2 * DIFF_DH
DIFF_K_W = DIFF_KV_HEADS * 2 * DIFF_DH
DIFF_V_W = DIFF_KV_HEADS * DIFF_DV

EVEN_IN_PAD = 2304
ODD_IN = DIFF_Q_W + DIFF_K_W + DIFF_V_W
LOG2E = 1.4426950408889634
NEG_BIG = -1e30

VMEM_LIMIT = 56 * 1024 * 1024


def _cparams(sem):
    return pltpu.CompilerParams(dimension_semantics=sem, vmem_limit_bytes=VMEM_LIMIT)


def _rms(x, g):
    return x * lax.rsqrt(jnp.mean(x * x, axis=-1, keepdims=True) + EPS) * g


def _silu(x):
    return x * (1.0 / (1.0 + jnp.exp(-x)))


def _swap_halves64(x):
    lane = lax.broadcasted_iota(jnp.int32, x.shape, 1)
    return jnp.where((lane & 63) < 32, pltpu.roll(x, 96, axis=1), pltpu.roll(x, 32, axis=1))


def _rope128(x, cos, sin):
    return x * cos + _swap_halves64(x) * sin


def _table_spec(table, tm):
    n_tiles = table.shape[0] // tm
    return pl.BlockSpec((tm, LANES), lambda i: (i % n_tiles, 0))


def _dot(a, b):
    return jnp.dot(a, b, preferred_element_type=F32)


def _dot_nt(a, b):
    return lax.dot_general(a, b, (((1,), (1,)), ((), ())), preferred_element_type=F32)


def _even_proj_kernel(y_ref, cos_ref, sin_ref, gpre_ref, win_ref, walpha_ref, balpha_ref, qnorm_ref, kvnorm_ref,
                      wuq_ref, wukt_ref,
                      qg_ref, kg_ref, vg_ref, rg_ref, la_ref, ckv_ref, krope_ref, qcat_ref, kcat_ref):
    h = _rms(y_ref[...], gpre_ref[...]).astype(BF16)
    p = _dot(h, win_ref[...])
    qg_ref[...] = p[:, 0:256] * (GLA_DK ** -0.5)
    kg_ref[...] = p[:, 256:512]
    vg_ref[...] = p[:, 512:1024]
    rg_ref[...] = p[:, 1024:1536]
    cos = cos_ref[...]
    sin = sin_ref[...]
    lane = lax.broadcasted_iota(jnp.int32, cos.shape, 1)
    low = lane < MLA_ROPE_DIM

    tail = p[:, 2176:2304]
    z = _dot(tail.astype(BF16), walpha_ref[...]) + balpha_ref[...]
    la_ref[...] = (jnp.minimum(z, 0.0) - jnp.log(1.0 + jnp.exp(-jnp.abs(z)))) * (1.0 / GLA_GATE_TEMP)
    kr = _rope128(tail, cos, sin)
    krope_ref[...] = kr[:, :MLA_ROPE_DIM]

    ckv = _rms(p[:, 1920:2176], kvnorm_ref[...])
    ckv_ref[...] = ckv
    kcat_ref[:, 0:MLA_KV_RANK] = ckv.astype(BF16)
    kcat_ref[:, MLA_KV_RANK:MLA_QK_PAD] = jnp.where(low, kr, 0.0).astype(BF16)

    cq = _rms(p[:, 1536:1920], qnorm_ref[...]).astype(BF16)
    q = _dot(cq, wuq_ref[...])
    sm = (MLA_NOPE_DIM + MLA_ROPE_DIM) ** -0.5 * LOG2E
    for s in range(MLA_HEADS // 2):
        slab = _rope128(q[:, 512 + 128 * s:640 + 128 * s], cos, sin) * sm
        for hh in range(2):
            head = 2 * s + hh
            q_lat = _dot(q[:, 128 * head:128 * head + 128].astype(BF16), wukt_ref[head]) * sm
            qcat_ref[head, :, 0:MLA_KV_RANK] = q_lat.astype(BF16)
            rp = slab if hh == 0 else pltpu.roll(slab, 64, axis=1)
            qcat_ref[head, :, MLA_KV_RANK:MLA_QK_PAD] = jnp.where(low, rp, 0.0).astype(BF16)


def _even_proj(y, cos, sin, w, tm):
    t = y.shape[0]
    grid = (t // tm,)
    row = lambda n: pl.BlockSpec((tm, n), lambda i: (i, 0))
    full2 = lambda a: pl.BlockSpec(a.shape, lambda i: (0, 0))
    full3 = lambda a: pl.BlockSpec(a.shape, lambda i: (0, 0, 0))
    tab = _table_spec(cos, tm)
    outs = [
        jax.ShapeDtypeStruct((t, 256), F32), jax.ShapeDtypeStruct((t, 256), F32),
        jax.ShapeDtypeStruct((t, 512), F32), jax.ShapeDtypeStruct((t, 512), F32),
        jax.ShapeDtypeStruct((t, 256), F32),
        jax.ShapeDtypeStruct((t, MLA_KV_RANK), F32), jax.ShapeDtypeStruct((t, MLA_ROPE_DIM), F32),
        jax.ShapeDtypeStruct((MLA_HEADS, t, MLA_QK_PAD), BF16), jax.ShapeDtypeStruct((t, MLA_QK_PAD), BF16),
    ]
    out_specs = [row(256), row(256), row(512), row(512), row(256), row(MLA_KV_RANK), row(MLA_ROPE_DIM),
                 pl.BlockSpec((MLA_HEADS, tm, MLA_QK_PAD), lambda i: (0, i, 0)), row(MLA_QK_PAD)]
    return pl.pallas_call(
        _even_proj_kernel,
        grid=grid,
        in_specs=[row(D_MODEL), tab, tab, full2(w["g_pre"]), full2(w["w_in"]), full2(w["w_alpha"]),
                  full2(w["b_alpha"]), full2(w["q_norm"]), full2(w["kv_norm"]), full2(w["w_uq"]), full3(w["w_ukt"])],
        out_specs=out_specs,
        out_shape=outs,
        compiler_params=_cparams(("parallel",)),
        name="even_proj",
    )(y, cos, sin, w["g_pre"], w["w_in"], w["w_alpha"], w["b_alpha"], w["q_norm"], w["kv_norm"], w["w_uq"], w["w_ukt"])


def _gla_prompt_kernel(q_ref, k_ref, la_ref, v_ref, o_ref, sfin_ref, s_ref):
    c = pl.program_id(1)
    C = GLA_CHUNK
    hi = lax.Precision.HIGHEST

    @pl.when(c == 0)
    def _():
        s_ref[...] = jnp.zeros_like(s_ref)

    row = lax.broadcasted_iota(jnp.int32, (C, C), 0)
    col = lax.broadcasted_iota(jnp.int32, (C, C), 1)
    tri = (col <= row).astype(F32)
    sub_start = (col < (row & -GLA_SUB)).astype(F32)
    ones = jnp.ones((C, LANES), F32)
    lane = lax.broadcasted_iota(jnp.int32, (C, LANES), 1)
    tloc = lax.broadcasted_iota(jnp.int32, (C, LANES), 0) & (GLA_SUB - 1)
    for slab in range(GLA_HEADS // 2):
        seg = slice(LANES * slab, LANES * (slab + 1))
        la = la_ref[:, seg]
        q = q_ref[:, seg]
        k = k_ref[:, seg]
        cum = jnp.dot(tri, la, preferred_element_type=F32, precision=hi)
        ref = jnp.dot(sub_start, la, preferred_element_type=F32, precision=hi)
        g_last = cum[C - 1:C, :]
        s_old = s_ref[slab]
        s_old_b = s_old.astype(BF16)
        q_state = q * jnp.exp(cum)
        q_off = q * jnp.exp(cum - ref)
        k_dec = k * jnp.exp(g_last - cum)
        diag = [q * k]
        for d in range(1, GLA_SUB):
            diag.append(jnp.where(tloc >= d,
                                  q * pltpu.roll(k, d, axis=0) * jnp.exp(cum - pltpu.roll(cum, d, axis=0)), 0.0))
        dec = jnp.exp(lax.dot_general(la, ones, (((0,), (0,)), ((), ())), preferred_element_type=F32, precision=hi))
        s_new = dec * s_old
        k_off = [(k * jnp.exp(jnp.minimum(ref[GLA_SUB * i:GLA_SUB * i + 1, :] - cum, 0.0))).astype(BF16)
                 for i in range(1, C // GLA_SUB)]
        for hh in range(2):
            head = 2 * slab + hh
            mine = (lane >= GLA_DK * hh) & (lane < GLA_DK * (hh + 1))
            v = v_ref[:, GLA_DV * head:GLA_DV * (head + 1)]
            vb = v.astype(BF16)
            o = _dot(jnp.where(mine, q_state, 0.0).astype(BF16), s_old_b)
            qt = jnp.where(mine, q_off, 0.0).astype(BF16)
            a_off = jnp.zeros((C, C), F32)
            for i in range(1, C // GLA_SUB):
                in_block = (row >= GLA_SUB * i) & (row < GLA_SUB * (i + 1)) & (col < GLA_SUB * i)
                a_off = jnp.where(in_block, _dot_nt(qt, k_off[i - 1]), a_off)
            o = o + _dot(a_off.astype(BF16), vb)
            for d in range(GLA_SUB):
                a_d = jnp.sum(jnp.where(mine, diag[d], 0.0), axis=1, keepdims=True)
                o = o + a_d * (v if d == 0 else pltpu.roll(v, d, axis=0))
            o_ref[:, GLA_DV * head:GLA_DV * (head + 1)] = o
            s_new = s_new + lax.dot_general(jnp.where(mine, k_dec, 0.0).astype(BF16), vb, (((0,), (0,)), ((), ())),
                                            preferred_element_type=F32)
        s_ref[slab] = s_new

    @pl.when(c == pl.num_programs(1) - 1)
    def _():
        sfin_ref[0] = s_ref[...]


def _gla_prompt(qg, kg, la, vg, batch):
    t = qg.shape[0]
    nc = t // batch // GLA_CHUNK
    n_slab = GLA_HEADS // 2
    row = lambda n: pl.BlockSpec((GLA_CHUNK, n), lambda b, c: (b * nc + c, 0))
    o, s_fin = pl.pallas_call(
        _gla_prompt_kernel,
        grid=(batch, nc),
        in_specs=[row(256), row(256), row(256), row(512)],
        out_specs=[row(512), pl.BlockSpec((1, n_slab, 2 * GLA_DK, GLA_DV), lambda b, c: (b, 0, 0, 0))],
        out_shape=[jax.ShapeDtypeStruct((t, GLA_V_W), F32),
                   jax.ShapeDtypeStruct((batch, n_slab, 2 * GLA_DK, GLA_DV), F32)],
        scratch_shapes=[pltpu.VMEM((n_slab, 2 * GLA_DK, GLA_DV), F32)],
        compiler_params=_cparams(("parallel", "arbitrary")),
        name="gla_prompt",
    )(qg, kg, la, vg)
    return o, s_fin.reshape(batch, GLA_HEADS, GLA_DK, GLA_DV)


GLA_DEC_BLOCK = 16


def _gla_sample_kernel(qt_ref, kt_ref, lat_ref, v_ref, s_ref, o_ref, snew_ref):
    for i in range(GLA_DEC_BLOCK):
        for h in range(GLA_HEADS):
            rows = slice(GLA_DK * h, GLA_DK * (h + 1))
            qc = qt_ref[0, rows, i:i + 1]
            kc = kt_ref[0, rows, i:i + 1]
            ac = jnp.exp(lat_ref[0, rows, i:i + 1])
            vr = v_ref[i:i + 1, GLA_DV * h:GLA_DV * (h + 1)]
            s_new = ac * s_ref[i, h] + kc * vr
            snew_ref[i, h] = s_new
            o_ref[i:i + 1, GLA_DV * h:GLA_DV * (h + 1)] = jnp.sum(qc * s_new, axis=0, keepdims=True)


def _gla_sample(qg, kg, la, vg, states, layer):
    n = qg.shape[0]
    nb = n // GLA_DEC_BLOCK
    cols = lambda a: a.reshape(nb, GLA_DEC_BLOCK, GLA_QK_W).transpose(0, 2, 1)
    colspec = pl.BlockSpec((1, GLA_QK_W, GLA_DEC_BLOCK), lambda i: (i, 0, 0))
    sblock = (GLA_DEC_BLOCK, GLA_HEADS, GLA_DK, GLA_DV)
    return pl.pallas_call(
        _gla_sample_kernel,
        grid=(nb,),
        in_specs=[colspec, colspec, colspec, pl.BlockSpec((GLA_DEC_BLOCK, GLA_V_W), lambda i: (i, 0)),
                  pl.BlockSpec((None,) + sblock, lambda i: (layer, i, 0, 0, 0))],
        out_specs=[pl.BlockSpec((GLA_DEC_BLOCK, GLA_V_W), lambda i: (i, 0)),
                   pl.BlockSpec(sblock, lambda i: (i, 0, 0, 0))],
        out_shape=[jax.ShapeDtypeStruct((n, GLA_V_W), F32), jax.ShapeDtypeStruct(states.shape[1:], F32)],
        compiler_params=_cparams(("parallel",)),
        name="gla_sample",
    )(cols(qg), cols(kg), cols(la), vg, states)


FLASH_TQ = 256
FLASH_TK = 512


def _causal_flash(get_q, n_groups, qi, k_ref, get_v, m_ref, l_ref, acc_ref):
    tq, tk = FLASH_TQ, FLASH_TK
    m_ref[...] = jnp.full_like(m_ref, NEG_BIG)
    acc_ref[...] = jnp.zeros_like(acc_ref)
    if l_ref is not None:
        l_ref[...] = jnp.zeros_like(l_ref)

    def chunk(start, masked):
        k = k_ref[pl.ds(start, tk), :]
        v = get_v(k, start)
        if masked:
            tok = qi * tq + lax.broadcasted_iota(jnp.int32, (tq, tk), 0)
            key = start + lax.broadcasted_iota(jnp.int32, (tq, tk), 1)
            visible = key <= tok
        for g in range(n_groups):
            s = _dot_nt(get_q(g), k)
            if masked:
                s = jnp.where(visible, s, NEG_BIG)
            m_prev = m_ref[g]
            m_new = jnp.maximum(m_prev, jnp.max(s, axis=1, keepdims=True))
            alpha = jnp.exp2(m_prev - m_new)
            p = jnp.exp2(s - m_new)
            if l_ref is not None:
                l_ref[g] = alpha * l_ref[g] + jnp.sum(p, axis=1, keepdims=True)
            acc_ref[g] = alpha * acc_ref[g] + _dot(p.astype(BF16), v)
            m_ref[g] = m_new

    n_chunks = ((qi + 1) * tq + tk - 1) // tk

    def body(i, carry):
        chunk(pl.multiple_of(i * tk, tk), False)
        return carry

    lax.fori_loop(0, n_chunks - 1, body, 0)
    chunk(pl.multiple_of((n_chunks - 1) * tk, tk), True)


def _mla_prompt_kernel(q_ref, k_ref, o_ref, m_ref, l_ref, acc_ref):
    _causal_flash(lambda g: q_ref[g], MLA_HEADS, pl.program_id(1), k_ref, lambda k, start: k[:, :MLA_KV_RANK],
                  m_ref, l_ref, acc_ref)
    o_ref[...] = acc_ref[...] / l_ref[...]


def _mla_prompt(qcat, kcat, batch):
    t = kcat.shape[0]
    seq = t // batch
    tq = FLASH_TQ
    assert seq % FLASH_TK == 0
    nq = seq // tq
    return pl.pallas_call(
        _mla_prompt_kernel,
        grid=(batch, nq),
        in_specs=[pl.BlockSpec((MLA_HEADS, tq, MLA_QK_PAD), lambda b, i: (0, b * nq + i, 0)),
                  pl.BlockSpec((seq, MLA_QK_PAD), lambda b, i: (b, 0))],
        out_specs=pl.BlockSpec((MLA_HEADS, tq, MLA_KV_RANK), lambda b, i: (0, b * nq + i, 0)),
        out_shape=jax.ShapeDtypeStruct((MLA_HEADS, t, MLA_KV_RANK), F32),
        scratch_shapes=[pltpu.VMEM((MLA_HEADS, tq, 1), F32), pltpu.VMEM((MLA_HEADS, tq, 1), F32),
                        pltpu.VMEM((MLA_HEADS, tq, MLA_KV_RANK), F32)],
        compiler_params=_cparams(("parallel", "arbitrary")),
        name="mla_prompt",
    )(qcat, kcat)


MLA_DEC_PAGES = 64
DEC_ROWS = 8


def _mla_sample_kernel(pt_ref, q_ref, knew_ref, *refs, pp):
    ckv_refs = refs[:pp]
    krt_refs = refs[pp:2 * pp]
    o_ref, m_ref, l_ref, acc_ref = refs[2 * pp:]
    c = pl.program_id(1)

    @pl.when(c == 0)
    def _():
        m_ref[...] = jnp.full_like(m_ref, NEG_BIG)
        l_ref[...] = jnp.zeros_like(l_ref)
        acc_ref[...] = jnp.zeros_like(acc_ref)

    q = q_ref[0]
    q_lat = q[:, :MLA_KV_RANK]
    q_rope = q[:, MLA_KV_RANK:MLA_KV_RANK + MLA_ROPE_DIM]
    pages = [r[...].astype(BF16) for r in ckv_refs]
    s = jnp.concatenate(
        [_dot_nt(q_lat, pages[i]) + _dot(q_rope, krt_refs[i][...].astype(BF16)) for i in range(pp)], axis=1)
    m_prev = m_ref[...]
    m_new = jnp.maximum(m_prev, jnp.max(s, axis=1, keepdims=True))
    alpha = jnp.exp2(m_prev - m_new)
    p = jnp.exp2(s - m_new)
    l_ref[...] = alpha * l_ref[...] + jnp.sum(p, axis=1, keepdims=True)
    pb = p.astype(BF16)
    psz = pages[0].shape[0]
    pv = _dot(pb[:, :psz], pages[0])
    for i in range(1, pp):
        pv = pv + _dot(pb[:, psz * i:psz * (i + 1)], pages[i])
    acc_ref[...] = alpha * acc_ref[...] + pv
    m_ref[...] = m_new

    @pl.when(c == pl.num_programs(1) - 1)
    def _():
        knew = knew_ref[0]
        s_new = jnp.sum(q.astype(F32) * knew.astype(F32), axis=1, keepdims=True)
        m_old = m_ref[...]
        m_fin = jnp.maximum(m_old, s_new)
        a = jnp.exp2(m_old - m_fin)
        p_new = jnp.exp2(s_new - m_fin)
        l_fin = a * l_ref[...] + p_new
        v_new = knew[:, :MLA_KV_RANK].astype(F32)
        o_ref[0] = (a * acc_ref[...] + p_new.astype(BF16).astype(F32) * v_new) / l_fin


def _mla_sample(page_table, q, knew, cache_ckv, cache_krt, layer):
    n, n_pages = page_table.shape
    pp = min(MLA_DEC_PAGES, n_pages)
    assert n_pages % pp == 0
    page = cache_ckv.shape[2]

    def pool_spec(shape, i):
        return pl.BlockSpec((None, None) + shape, lambda b, c, pt: (layer, pt[b * n_pages + c * pp + i], 0, 0))

    grid_spec = pltpu.PrefetchScalarGridSpec(
        num_scalar_prefetch=1,
        grid=(n, n_pages // pp),
        in_specs=([pl.BlockSpec((1, DEC_ROWS, MLA_QK_PAD), lambda b, c, pt: (b, 0, 0)),
                   pl.BlockSpec((1, 1, MLA_QK_PAD), lambda b, c, pt: (b, 0, 0))]
                  + [pool_spec((page, MLA_KV_RANK), i) for i in range(pp)]
                  + [pool_spec((MLA_ROPE_DIM, page), i) for i in range(pp)]),
        out_specs=pl.BlockSpec((1, DEC_ROWS, MLA_KV_RANK), lambda b, c, pt: (b, 0, 0)),
        scratch_shapes=[pltpu.VMEM((DEC_ROWS, 1), F32), pltpu.VMEM((DEC_ROWS, 1), F32),
                        pltpu.VMEM((DEC_ROWS, MLA_KV_RANK), F32)],
    )
    return pl.pallas_call(
        functools.partial(_mla_sample_kernel, pp=pp),
        grid_spec=grid_spec,
        out_shape=jax.ShapeDtypeStruct((n, DEC_ROWS, MLA_KV_RANK), F32),
        compiler_params=_cparams(("parallel", "arbitrary")),
        name="mla_sample",
    )(page_table.reshape(-1), q, knew, *([cache_ckv] * pp), *([cache_krt] * pp))


def _even_mix_kernel(y_ref, og_ref, rg_ref, olat_ref, glanorm_ref, wuv_ref, wout_ref, gpost_ref, out_ref):
    pieces = []
    for h in range(GLA_HEADS):
        seg = slice(GLA_DV * h, GLA_DV * (h + 1))
        pieces.append((_rms(og_ref[:, seg], glanorm_ref[...]) * _silu(rg_ref[:, seg])).astype(BF16))
    for h in range(MLA_HEADS):
        pieces.append(_dot(olat_ref[h].astype(BF16), wuv_ref[h]).astype(BF16))
    mix = jnp.concatenate(pieces, axis=1)
    out_ref[...] = y_ref[...] + _rms(_dot(mix, wout_ref[...]), gpost_ref[...])


def _even_mix(y, og, rg, olat, w, tm):
    t = y.shape[0]
    row = lambda n: pl.BlockSpec((tm, n), lambda i: (i, 0))
    full2 = lambda a: pl.BlockSpec(a.shape, lambda i: (0, 0))
    full3 = lambda a: pl.BlockSpec(a.shape, lambda i: (0, 0, 0))
    return pl.pallas_call(
        _even_mix_kernel,
        grid=(t // tm,),
        in_specs=[row(D_MODEL), row(GLA_V_W), row(GLA_V_W),
                  pl.BlockSpec((MLA_HEADS, tm, MLA_KV_RANK), lambda i: (0, i, 0)),
                  full2(w["gla_norm"]), full3(w["w_uv"]), full2(w["w_out"]), full2(w["g_post"])],
        out_specs=row(D_MODEL),
        out_shape=jax.ShapeDtypeStruct((t, D_MODEL), F32),
        compiler_params=_cparams(("parallel",)),
        name="even_mix",
    )(y, og, rg, olat, w["gla_norm"], w["w_uv"], w["w_out"], w["g_post"])


def _odd_proj_kernel(y_ref, cos_ref, sin_ref, gpre_ref, win_ref, dk_ref, dv_ref, qd_ref, kd_ref, vd_ref):
    h = _rms(y_ref[...], gpre_ref[...]).astype(BF16)
    p = _dot(h, win_ref[...])
    cos = cos_ref[...]
    sin = sin_ref[...]
    lane = lax.broadcasted_iota(jnp.int32, cos.shape, 1)
    low = lane < DIFF_DH
    sm = DIFF_DH ** -0.5 * LOG2E
    for head in range(DIFF_HEADS):
        g, r = divmod(head, DIFF_REP)
        slab = _rope128(p[:, LANES * head:LANES * (head + 1)], cos, sin) * sm
        qd_ref[g, r] = jnp.where(low, slab, 0.0).astype(BF16)
        qd_ref[g, DIFF_REP + r] = jnp.where(low, 0.0, slab).astype(BF16)
    for g in range(DIFF_KV_HEADS):
        seg = slice(LANES * g, LANES * (g + 1))
        kk = _rope128(p[:, DIFF_Q_W + LANES * g:DIFF_Q_W + LANES * (g + 1)], cos, sin)
        dk_ref[:, seg] = kk
        kd_ref[:, seg] = kk.astype(BF16)
    vv = p[:, DIFF_Q_W + DIFF_K_W:ODD_IN]
    dv_ref[...] = vv
    vd_ref[...] = vv.astype(BF16)


def _odd_proj(y, cos, sin, w, tm):
    t = y.shape[0]
    row = lambda n: pl.BlockSpec((tm, n), lambda i: (i, 0))
    full2 = lambda a: pl.BlockSpec(a.shape, lambda i: (0, 0))
    return pl.pallas_call(
        _odd_proj_kernel,
        grid=(t // tm,),
        in_specs=[row(D_MODEL), _table_spec(cos, tm), _table_spec(cos, tm), full2(w["g_pre"]), full2(w["w_in"])],
        out_specs=[row(DIFF_K_W), row(DIFF_V_W),
                   pl.BlockSpec((DIFF_KV_HEADS, 2 * DIFF_REP, tm, LANES), lambda i: (0, 0, i, 0)),
                   row(DIFF_K_W), row(DIFF_V_W)],
        out_shape=[jax.ShapeDtypeStruct((t, DIFF_K_W), F32), jax.ShapeDtypeStruct((t, DIFF_V_W), F32),
                   jax.ShapeDtypeStruct((DIFF_KV_HEADS, 2 * DIFF_REP, t, LANES), BF16),
                   jax.ShapeDtypeStruct((t, DIFF_K_W), BF16), jax.ShapeDtypeStruct((t, DIFF_V_W), BF16)],
        compiler_params=_cparams(("parallel",)),
        name="odd_proj",
    )(y, cos, sin, w["g_pre"], w["w_in"])


def _diff_lambda(lam_ref, lam_init):
    lq1, lk1, lq2, lk2 = lam_ref[0:1, :], lam_ref[1:2, :], lam_ref[2:3, :], lam_ref[3:4, :]
    return (jnp.exp(jnp.sum(lq1 * lk1, axis=1, keepdims=True))
            - jnp.exp(jnp.sum(lq2 * lk2, axis=1, keepdims=True)) + lam_init)


def _diff_prompt_kernel(lam_ref, q_ref, k_ref, v_ref, o_ref, m_ref, acc_ref, *, lam_init):
    def values(k, start):
        v = v_ref[pl.ds(start, FLASH_TK), :]
        return jnp.concatenate([v, jnp.ones_like(v)], axis=1)

    _causal_flash(lambda g: q_ref[0, g], 2 * DIFF_REP, pl.program_id(2), k_ref, values, m_ref, None, acc_ref)
    lam = _diff_lambda(lam_ref, lam_init)
    for r in range(DIFF_REP):
        a0 = acc_ref[r]
        a1 = acc_ref[DIFF_REP + r]
        o_ref[:, DIFF_DV * r:DIFF_DV * (r + 1)] = (
            a0[:, :DIFF_DV] / a0[:, DIFF_DV:] - lam * (a1[:, :DIFF_DV] / a1[:, DIFF_DV:]))


def _diff_prompt(qd, kd, vd, lam_vecs, lam_init, batch):
    t = kd.shape[0]
    seq = t // batch
    tq = FLASH_TQ
    assert seq % FLASH_TK == 0
    nq = seq // tq
    groups = 2 * DIFF_REP
    return pl.pallas_call(
        functools.partial(_diff_prompt_kernel, lam_init=lam_init),
        grid=(batch, DIFF_KV_HEADS, nq),
        in_specs=[pl.BlockSpec(lam_vecs.shape, lambda b, g, i: (0, 0)),
                  pl.BlockSpec((1, 2 * DIFF_REP, tq, LANES), lambda b, g, i: (g, 0, b * nq + i, 0)),
                  pl.BlockSpec((seq, LANES), lambda b, g, i: (b, g)),
                  pl.BlockSpec((seq, DIFF_DV), lambda b, g, i: (b, g))],
        out_specs=pl.BlockSpec((tq, DIFF_REP * DIFF_DV), lambda b, g, i: (b * nq + i, g)),
        out_shape=jax.ShapeDtypeStruct((t, DIFF_HEADS * DIFF_DV), F32),
        scratch_shapes=[pltpu.VMEM((groups, tq, 1), F32), pltpu.VMEM((groups, tq, 2 * DIFF_DV), F32)],
        compiler_params=_cparams(("parallel", "parallel", "arbitrary")),
        name="diff_prompt",
    )(lam_vecs, qd, kd, vd)


DIFF_DEC_ROWS = 2 * DIFF_HEADS
DIFF_DEC_PAGES = 32


def _diff_sample_kernel(pt_ref, lam_ref, q_ref, knew_ref, vnew_ref, *refs, lam_init, pp):
    k_refs = refs[:pp]
    v_refs = refs[pp:2 * pp]
    o_ref, m_ref, l_ref, acc_ref = refs[2 * pp:]
    c = pl.program_id(1)

    @pl.when(c == 0)
    def _():
        m_ref[...] = jnp.full_like(m_ref, NEG_BIG)
        l_ref[...] = jnp.zeros_like(l_ref)
        acc_ref[...] = jnp.zeros_like(acc_ref)

    q = q_ref[0]
    s = jnp.concatenate([_dot_nt(q, k_refs[i][...].astype(BF16)) for i in range(pp)], axis=1)
    row_head = lax.broadcasted_iota(jnp.int32, s.shape, 0) // (2 * DIFF_REP)
    col_head = lax.broadcasted_iota(jnp.int32, s.shape, 1) & (DIFF_KV_HEADS - 1)
    s = jnp.where(row_head == col_head, s, NEG_BIG)
    m_prev = m_ref[...]
    m_new = jnp.maximum(m_prev, jnp.max(s, axis=1, keepdims=True))
    alpha = jnp.exp2(m_prev - m_new)
    p = jnp.exp2(s - m_new)
    l_ref[...] = alpha * l_ref[...] + jnp.sum(p, axis=1, keepdims=True)
    pb = p.astype(BF16)
    psz = k_refs[0].shape[0]
    pv = _dot(pb[:, :psz], v_refs[0][...].astype(BF16))
    for i in range(1, pp):
        pv = pv + _dot(pb[:, psz * i:psz * (i + 1)], v_refs[i][...].astype(BF16))
    acc_ref[...] = alpha * acc_ref[...] + pv
    m_ref[...] = m_new

    @pl.when(c == pl.num_programs(1) - 1)
    def _():
        first = lax.broadcasted_iota(jnp.int32, (DIFF_DEC_ROWS, LANES), 0) < 2 * DIFF_REP
        knew = knew_ref[0].astype(F32)
        vnew = vnew_ref[0].astype(F32)
        k_row = jnp.where(first, knew[:, :LANES], knew[:, LANES:])
        v_row = jnp.where(first, vnew[:, :DIFF_DV], vnew[:, DIFF_DV:])
        s_new = jnp.sum(q.astype(F32) * k_row, axis=1, keepdims=True)
        m_old = m_ref[...]
        m_fin = jnp.maximum(m_old, s_new)
        a = jnp.exp2(m_old - m_fin)
        p_new = jnp.exp2(s_new - m_fin)
        l_fin = a * l_ref[...] + p_new
        o = (a * acc_ref[...] + p_new.astype(BF16).astype(F32) * v_row) / l_fin
        lam = _diff_lambda(lam_ref, lam_init)
        for g in range(DIFF_KV_HEADS):
            base = 2 * DIFF_REP * g
            o_ref[0, DIFF_REP * g:DIFF_REP * (g + 1), :] = (
                o[base:base + DIFF_REP] - lam * o[base + DIFF_REP:base + 2 * DIFF_REP])


def _diff_sample(page_table, lam_vecs, lam_init, q, knew, vnew, cache_k, cache_v, layer):
    n, n_pages = page_table.shape
    pp = min(DIFF_DEC_PAGES, n_pages)
    assert n_pages % pp == 0
    page_rows = cache_k.shape[2]

    def pool_spec(i):
        return pl.BlockSpec((None, None, page_rows, LANES),
                            lambda b, c, pt: (layer, pt[b * n_pages + c * pp + i], 0, 0))

    grid_spec = pltpu.PrefetchScalarGridSpec(
        num_scalar_prefetch=1,
        grid=(n, n_pages // pp),
        in_specs=([pl.BlockSpec(lam_vecs.shape, lambda b, c, pt: (0, 0)),
                   pl.BlockSpec((1, DIFF_DEC_ROWS, LANES), lambda b, c, pt: (b, 0, 0)),
                   pl.BlockSpec((1, 1, DIFF_K_W), lambda b, c, pt: (b, 0, 0)),
                   pl.BlockSpec((1, 1, DIFF_V_W), lambda b, c, pt: (b, 0, 0))]
                  + [pool_spec(i) for i in range(pp)] + [pool_spec(i) for i in range(pp)]),
        out_specs=pl.BlockSpec((1, DIFF_HEADS, DIFF_DV), lambda b, c, pt: (b, 0, 0)),
        scratch_shapes=[pltpu.VMEM((DIFF_DEC_ROWS, 1), F32), pltpu.VMEM((DIFF_DEC_ROWS, 1), F32),
                        pltpu.VMEM((DIFF_DEC_ROWS, DIFF_DV), F32)],
    )
    return pl.pallas_call(
        functools.partial(_diff_sample_kernel, lam_init=lam_init, pp=pp),
        grid_spec=grid_spec,
        out_shape=jax.ShapeDtypeStruct((n, DIFF_HEADS, DIFF_DV), F32),
        compiler_params=_cparams(("parallel", "arbitrary")),
        name="diff_sample",
    )(page_table.reshape(-1), lam_vecs, q, knew, vnew, *([cache_k] * pp), *([cache_v] * pp))


def _odd_mix_kernel(y_ref, o_ref, subln_ref, wout_ref, gpost_ref, out_ref, *, lam_init):
    pieces = []
    for h in range(DIFF_HEADS):
        seg = slice(DIFF_DV * h, DIFF_DV * (h + 1))
        pieces.append((_rms(o_ref[:, seg], subln_ref[...]) * (1.0 - lam_init)).astype(BF16))
    mix = jnp.concatenate(pieces, axis=1)
    out_ref[...] = y_ref[...] + _rms(_dot(mix, wout_ref[...]), gpost_ref[...])


def _odd_mix(y, o, w, lam_init, tm):
    t = y.shape[0]
    row = lambda n: pl.BlockSpec((tm, n), lambda i: (i, 0))
    full2 = lambda a: pl.BlockSpec(a.shape, lambda i: (0, 0))
    return pl.pallas_call(
        functools.partial(_odd_mix_kernel, lam_init=lam_init),
        grid=(t // tm,),
        in_specs=[row(D_MODEL), row(D_MODEL), full2(w["subln"]), full2(w["w_out"]), full2(w["g_post"])],
        out_specs=row(D_MODEL),
        out_shape=jax.ShapeDtypeStruct((t, D_MODEL), F32),
        compiler_params=_cparams(("parallel",)),
        name="odd_mix",
    )(y, o, w["subln"], w["w_out"], w["g_post"])


def _ffn_kernel(y_ref, gpre_ref, wg_ref, wu_ref, wd_ref, gpost_ref, out_ref, h_ref, acc_ref):
    f = pl.program_id(1)

    @pl.when(f == 0)
    def _():
        h_ref[...] = _rms(y_ref[...], gpre_ref[...]).astype(BF16)
        acc_ref[...] = jnp.zeros_like(acc_ref)

    h = h_ref[...]
    a = (_silu(_dot(h, wg_ref[...])) * _dot(h, wu_ref[...])).astype(BF16)
    acc_ref[...] += _dot(a, wd_ref[...])

    @pl.when(f == pl.num_programs(1) - 1)
    def _():
        out_ref[...] = y_ref[...] + _rms(acc_ref[...], gpost_ref[...])


def _ffn(y, w, tm):
    t = y.shape[0]
    d_ff = w["w_gate"].shape[1]
    tf = d_ff // 2 if (d_ff // 2) % LANES == 0 else d_ff
    return pl.pallas_call(
        _ffn_kernel,
        grid=(t // tm, d_ff // tf),
        in_specs=[pl.BlockSpec((tm, D_MODEL), lambda i, f: (i, 0)),
                  pl.BlockSpec((1, D_MODEL), lambda i, f: (0, 0)),
                  pl.BlockSpec((D_MODEL, tf), lambda i, f: (0, f)),
                  pl.BlockSpec((D_MODEL, tf), lambda i, f: (0, f)),
                  pl.BlockSpec((tf, D_MODEL), lambda i, f: (f, 0)),
                  pl.BlockSpec((1, D_MODEL), lambda i, f: (0, 0))],
        out_specs=pl.BlockSpec((tm, D_MODEL), lambda i, f: (i, 0)),
        out_shape=jax.ShapeDtypeStruct((t, D_MODEL), F32),
        scratch_shapes=[pltpu.VMEM((tm, D_MODEL), BF16), pltpu.VMEM((tm, D_MODEL), F32)],
        compiler_params=_cparams(("parallel", "arbitrary")),
        name="ffn",
    )(y, w["g_pre"], w["w_gate"], w["w_up"], w["w_down"], w["g_post"])


def _rope_tables(pos):
    half = MLA_ROPE_DIM // 2
    inv = ROPE_THETA ** (-jnp.arange(0, MLA_ROPE_DIM, 2, dtype=F32) / MLA_ROPE_DIM)
    ang = pos.astype(F32)[:, None] * inv[None, :]
    cos, sin = jnp.cos(ang), jnp.sin(ang)
    assert cos.shape[1] == half
    return jnp.tile(cos, (1, 4)), jnp.tile(jnp.concatenate([-sin, sin], axis=1), (1, 2))


def _row(v):
    return v.reshape(1, -1).astype(F32)


def _pack_even(j, norm_mix_pre_i, norm_mix_post_i, w_in_even, w_gla_alpha, b_gla_alpha, gla_out_norm, mla_q_norm,
               w_mla_uq, mla_kv_norm, w_mla_uk, w_mla_uv, w_out_even):
    w_in = w_in_even[j]
    widths = [GLA_QK_W, GLA_QK_W, GLA_V_W, GLA_GATE_RANK, GLA_V_W, MLA_Q_RANK, MLA_KV_RANK, MLA_ROPE_DIM]
    offs = np.concatenate([[0], np.cumsum(widths)])
    seg = lambda n: w_in[:, int(offs[n]):int(offs[n + 1])]
    pad = jnp.zeros((D_MODEL, EVEN_IN_PAD - int(offs[-1])), w_in.dtype)
    packed = jnp.concatenate([seg(0), seg(1), seg(2), seg(4), seg(5), seg(6), seg(7), seg(3), pad], axis=1)
    w_alpha = jnp.zeros((LANES, GLA_QK_W), F32).at[MLA_ROPE_DIM:MLA_ROPE_DIM + GLA_GATE_RANK].set(w_gla_alpha[j])
    uq = w_mla_uq[j].reshape(MLA_Q_RANK, MLA_HEADS, MLA_NOPE_DIM + MLA_ROPE_DIM)
    uq = jnp.concatenate([uq[:, :, :MLA_NOPE_DIM].reshape(MLA_Q_RANK, -1),
                          uq[:, :, MLA_NOPE_DIM:].reshape(MLA_Q_RANK, -1)], axis=1)
    return {
        "g_pre": _row(norm_mix_pre_i), "g_post": _row(norm_mix_post_i),
        "w_in": packed.astype(BF16), "w_alpha": w_alpha.astype(BF16), "b_alpha": _row(b_gla_alpha[j]),
        "q_norm": _row(mla_q_norm[j]), "kv_norm": _row(mla_kv_norm[j]), "w_uq": uq.astype(BF16),
        "w_ukt": jnp.transpose(w_mla_uk[j], (1, 2, 0)).astype(BF16),
        "w_uv": jnp.transpose(w_mla_uv[j], (1, 0, 2)).astype(BF16),
        "gla_norm": _row(gla_out_norm[j]), "w_out": w_out_even[j].astype(BF16),
    }


def _decode_q_mla(qcat):
    q = jnp.transpose(qcat, (1, 0, 2))
    return jnp.pad(q, ((0, 0), (0, DEC_ROWS - MLA_HEADS), (0, 0)))


def _decode_q_diff(qd):
    n = qd.shape[2]
    return jnp.transpose(qd, (2, 0, 1, 3)).reshape(n, DIFF_DEC_ROWS, LANES)


def kernel(x_prompt, x_sample, state_gla, cache_mla_ckv, cache_mla_krope, cache_diff_k, cache_diff_v, page_table,
           norm_mix_pre, norm_mix_post, norm_ffn_pre, norm_ffn_post,
           w_in_even, w_gla_alpha, b_gla_alpha, gla_out_norm, mla_q_norm, w_mla_uq, mla_kv_norm, w_mla_uk, w_mla_uv,
           w_out_even,
           w_in_odd, diff_lambda_q1, diff_lambda_k1, diff_lambda_q2, diff_lambda_k2, diff_subln, w_out_odd,
           w_ffn_gate, w_ffn_up, w_ffn_down):
    batch, seq, _ = x_prompt.shape
    n_dec, dec_seq, _ = x_sample.shape
    assert dec_seq == 1
    depth = norm_mix_pre.shape[0]
    n_pages = page_table.shape[1]
    page = cache_mla_ckv.shape[2]
    past_len = n_pages * page
    tp = batch * seq
    tm_p = min(512, seq)
    tm_s = n_dec

    cos_p, sin_p = _rope_tables(jnp.arange(seq, dtype=jnp.int32))
    cos_s, sin_s = _rope_tables(jnp.full((n_dec,), past_len, jnp.int32))

    cache_krt = jnp.transpose(cache_mla_krope, (0, 1, 3, 2))
    cache_dk = cache_diff_k.reshape(cache_diff_k.shape[:2] + (page * DIFF_KV_HEADS, 2 * DIFF_DH))
    cache_dv = cache_diff_v.reshape(cache_diff_v.shape[:2] + (page * DIFF_KV_HEADS, DIFF_DV))

    yp = x_prompt.reshape(tp, D_MODEL)
    ys = x_sample.reshape(n_dec, D_MODEL)
    gla_p, gla_s, ckv_p, kr_p, ckv_s, kr_s = [], [], [], [], [], []
    dk_p, dv_p, dk_s, dv_s = [], [], [], []
    for i in range(depth):
        j = i // 2
        if i % 2 == 0:
            w = _pack_even(j, norm_mix_pre[i], norm_mix_post[i], w_in_even, w_gla_alpha, b_gla_alpha, gla_out_norm,
                           mla_q_norm, w_mla_uq, mla_kv_norm, w_mla_uk, w_mla_uv, w_out_even)
            qg, kg, vg, rg, la, ckv, krope, qcat, kcat = _even_proj(yp, cos_p, sin_p, w, tm_p)
            og, s_fin = _gla_prompt(qg, kg, la, vg, batch)
            olat = _mla_prompt(qcat, kcat, batch)
            yp = _even_mix(yp, og, rg, olat, w, tm_p)
            gla_p.append(s_fin)
            ckv_p.append(ckv.reshape(batch, seq, MLA_KV_RANK))
            kr_p.append(krope.reshape(batch, seq, MLA_ROPE_DIM))

            qg, kg, vg, rg, la, ckv, krope, qcat, kcat = _even_proj(ys, cos_s, sin_s, w, tm_s)
            og, s_new = _gla_sample(qg, kg, la, vg, state_gla, j)
            olat = _mla_sample(page_table, _decode_q_mla(qcat), kcat.reshape(n_dec, 1, MLA_QK_PAD),
                               cache_mla_ckv, cache_krt, j)
            olat = jnp.transpose(olat[:, :MLA_HEADS], (1, 0, 2))
            ys = _even_mix(ys, og, rg, olat, w, tm_s)
            gla_s.append(s_new)
            ckv_s.append(ckv.reshape(n_dec, 1, MLA_KV_RANK))
            kr_s.append(krope.reshape(n_dec, 1, MLA_ROPE_DIM))
        else:
            lam_init = 0.8 - 0.6 * math.exp(-0.3 * i)
            w = {"g_pre": _row(norm_mix_pre[i]), "g_post": _row(norm_mix_post[i]), "w_in": w_in_odd[j].astype(BF16),
                 "subln": _row(diff_subln[j]), "w_out": w_out_odd[j].astype(BF16)}
            lam_vecs = jnp.stack([diff_lambda_q1[j], diff_lambda_k1[j], diff_lambda_q2[j], diff_lambda_k2[j]]).astype(F32)
            dk, dv, qd, kd, vd = _odd_proj(yp, cos_p, sin_p, w, tm_p)
            o = _diff_prompt(qd, kd, vd, lam_vecs, lam_init, batch)
            yp = _odd_mix(yp, o, w, lam_init, tm_p)
            dk_p.append(dk.reshape(batch, seq, DIFF_KV_HEADS, 2 * DIFF_DH))
            dv_p.append(dv.reshape(batch, seq, DIFF_KV_HEADS, DIFF_DV))

            dk, dv, qd, kd, vd = _odd_proj(ys, cos_s, sin_s, w, tm_s)
            o = _diff_sample(page_table, lam_vecs, lam_init, _decode_q_diff(qd), kd.reshape(n_dec, 1, DIFF_K_W),
                             vd.reshape(n_dec, 1, DIFF_V_W), cache_dk, cache_dv, j)
            ys = _odd_mix(ys, o.reshape(n_dec, DIFF_HEADS * DIFF_DV), w, lam_init, tm_s)
            dk_s.append(dk.reshape(n_dec, 1, DIFF_KV_HEADS, 2 * DIFF_DH))
            dv_s.append(dv.reshape(n_dec, 1, DIFF_KV_HEADS, DIFF_DV))
        wf = {"g_pre": _row(norm_ffn_pre[i]), "g_post": _row(norm_ffn_post[i]), "w_gate": w_ffn_gate[i].astype(BF16),
              "w_up": w_ffn_up[i].astype(BF16), "w_down": w_ffn_down[i].astype(BF16)}
        yp = _ffn(yp, wf, tm_p)
        ys = _ffn(ys, wf, tm_s)
    return (yp.reshape(batch, seq, D_MODEL), ys.reshape(n_dec, 1, D_MODEL),
            jnp.stack(gla_p), jnp.stack(gla_s),
            jnp.stack(ckv_p), jnp.stack(kr_p), jnp.stack(ckv_s), jnp.stack(kr_s),
            jnp.stack(dk_p), jnp.stack(dv_p), jnp.stack(dk_s), jnp.stack(dv_s))
```

```python
import functools
import math

import jax
import jax.numpy as jnp
import numpy as np
from jax import lax
from jax.experimental import pallas as pl
from jax.experimental.pallas import tpu as pltpu

F32 = jnp.float32
BF16 = jnp.bfloat16

D_MODEL = 1024
EPS = 1e-6
ROPE_THETA = 10000.0
LANES = 128

GLA_HEADS = 4
GLA_DK = 64
GLA_DV = 128
GLA_QK_W = GLA_HEADS * GLA_DK
GLA_V_W = GLA_HEADS * GLA_DV
GLA_GATE_RANK = 16
GLA_GATE_TEMP = 16.0
GLA_CHUNK = 64
GLA_SUB = 16

MLA_HEADS = 4
MLA_Q_RANK = 384
MLA_KV_RANK = 256
MLA_NOPE_DIM = 128
MLA_ROPE_DIM = 64
MLA_V_DIM = 128
MLA_QK_PAD = MLA_KV_RANK + LANES

DIFF_HEADS = 8
DIFF_KV_HEADS = 2
DIFF_REP = DIFF_HEADS // DIFF_KV_HEADS
DIFF_DH = 64
DIFF_DV = 2 * DIFF_DH
DIFF_Q_W = DIFF_HEADS * 2 * DIFF_DH
DIFF_K_W = DIFF_KV_HEADS * 2 * DIFF_DH
DIFF_V_W = DIFF_KV_HEADS * DIFF_DV

EVEN_IN_PAD = 2304
ODD_IN = DIFF_Q_W + DIFF_K_W + DIFF_V_W
LOG2E = 1.4426950408889634
NEG_BIG = -1e30

VMEM_LIMIT = 56 * 1024 * 1024


def _cparams(sem):
    return pltpu.CompilerParams(dimension_semantics=sem, vmem_limit_bytes=VMEM_LIMIT)


def _rms(x, g):
    return x * lax.rsqrt(jnp.mean(x * x, axis=-1, keepdims=True) + EPS) * g


def _silu(x):
    return x * (1.0 / (1.0 + jnp.exp(-x)))


def _swap_halves64(x):
    lane = lax.broadcasted_iota(jnp.int32, x.shape, 1)
    return jnp.where((lane & 63) < 32, pltpu.roll(x, 96, axis=1), pltpu.roll(x, 32, axis=1))


def _rope128(x, cos, sin):
    return x * cos + _swap_halves64(x) * sin


def _table_spec(table, tm):
    n_tiles = table.shape[0] // tm
    return pl.BlockSpec((tm, LANES), lambda i: (i % n_tiles, 0))


def _dot(a, b):
    return jnp.dot(a, b, preferred_element_type=F32)


def _dot_nt(a, b):
    return lax.dot_general(a, b, (((1,), (1,)), ((), ())), preferred_element_type=F32)


def _even_proj_kernel(y_ref, cos_ref, sin_ref, gpre_ref, win_ref, walpha_ref, balpha_ref, qnorm_ref, kvnorm_ref,
                      wuq_ref, wukt_ref,
                      qg_ref, kg_ref, vg_ref, rg_ref, la_ref, ckv_ref, krope_ref, qcat_ref, kcat_ref):
    h = _rms(y_ref[...], gpre_ref[...]).astype(BF16)
    p = _dot(h, win_ref[...])
    qg_ref[...] = p[:, 0:256] * (GLA_DK ** -0.5)
    kg_ref[...] = p[:, 256:512]
    vg_ref[...] = p[:, 512:1024]
    rg_ref[...] = p[:, 1024:1536]
    cos = cos_ref[...]
    sin = sin_ref[...]
    lane = lax.broadcasted_iota(jnp.int32, cos.shape, 1)
    low = lane < MLA_ROPE_DIM

    tail = p[:, 2176:2304]
    z = _dot(tail.astype(BF16), walpha_ref[...]) + balpha_ref[...]
    la_ref[...] = (jnp.minimum(z, 0.0) - jnp.log(1.0 + jnp.exp(-jnp.abs(z)))) * (1.0 / GLA_GATE_TEMP)
    kr = _rope128(tail, cos, sin)
    krope_ref[...] = kr[:, :MLA_ROPE_DIM]

    ckv = _rms(p[:, 1920:2176], kvnorm_ref[...])
    ckv_ref[...] = ckv
    kcat_ref[:, 0:MLA_KV_RANK] = ckv.astype(BF16)
    kcat_ref[:, MLA_KV_RANK:MLA_QK_PAD] = jnp.where(low, kr, 0.0).astype(BF16)

    cq = _rms(p[:, 1536:1920], qnorm_ref[...]).astype(BF16)
    q = _dot(cq, wuq_ref[...])
    sm = (MLA_NOPE_DIM + MLA_ROPE_DIM) ** -0.5 * LOG2E
    for s in range(MLA_HEADS // 2):
        slab = _rope128(q[:, 512 + 128 * s:640 + 128 * s], cos, sin) * sm
        for hh in range(2):
            head = 2 * s + hh
            q_lat = _dot(q[:, 128 * head:128 * head + 128].astype(BF16), wukt_ref[head]) * sm
            qcat_ref[head, :, 0:MLA_KV_RANK] = q_lat.astype(BF16)
            rp = slab if hh == 0 else pltpu.roll(slab, 64, axis=1)
            qcat_ref[head, :, MLA_KV_RANK:MLA_QK_PAD] = jnp.where(low, rp, 0.0).astype(BF16)


def _even_proj(y, cos, sin, w, tm):
    t = y.shape[0]
    grid = (t // tm,)
    row = lambda n: pl.BlockSpec((tm, n), lambda i: (i, 0))
    full2 = lambda a: pl.BlockSpec(a.shape, lambda i: (0, 0))
    full3 = lambda a: pl.BlockSpec(a.shape, lambda i: (0, 0, 0))
    tab = _table_spec(cos, tm)
    outs = [
        jax.ShapeDtypeStruct((t, 256), F32), jax.ShapeDtypeStruct((t, 256), F32),
        jax.ShapeDtypeStruct((t, 512), F32), jax.ShapeDtypeStruct((t, 512), F32),
        jax.ShapeDtypeStruct((t, 256), F32),
        jax.ShapeDtypeStruct((t, MLA_KV_RANK), F32), jax.ShapeDtypeStruct((t, MLA_ROPE_DIM), F32),
        jax.ShapeDtypeStruct((MLA_HEADS, t, MLA_QK_PAD), BF16), jax.ShapeDtypeStruct((t, MLA_QK_PAD), BF16),
    ]
    out_specs = [row(256), row(256), row(512), row(512), row(256), row(MLA_KV_RANK), row(MLA_ROPE_DIM),
                 pl.BlockSpec((MLA_HEADS, tm, MLA_QK_PAD), lambda i: (0, i, 0)), row(MLA_QK_PAD)]
    return pl.pallas_call(
        _even_proj_kernel,
        grid=grid,
        in_specs=[row(D_MODEL), tab, tab, full2(w["g_pre"]), full2(w["w_in"]), full2(w["w_alpha"]),
                  full2(w["b_alpha"]), full2(w["q_norm"]), full2(w["kv_norm"]), full2(w["w_uq"]), full3(w["w_ukt"])],
        out_specs=out_specs,
        out_shape=outs,
        compiler_params=_cparams(("parallel",)),
        name="even_proj",
    )(y, cos, sin, w["g_pre"], w["w_in"], w["w_alpha"], w["b_alpha"], w["q_norm"], w["kv_norm"], w["w_uq"], w["w_ukt"])


def _gla_prompt_kernel(q_ref, k_ref, la_ref, v_ref, o_ref, sfin_ref, s_ref):
    c = pl.program_id(1)
    C = GLA_CHUNK
    hi = lax.Precision.HIGHEST

    @pl.when(c == 0)
    def _():
        s_ref[...] = jnp.zeros_like(s_ref)

    row = lax.broadcasted_iota(jnp.int32, (C, C), 0)
    col = lax.broadcasted_iota(jnp.int32, (C, C), 1)
    tri = (col <= row).astype(F32)
    sub_start = (col < (row & -GLA_SUB)).astype(F32)
    ones = jnp.ones((C, LANES), F32)
    lane = lax.broadcasted_iota(jnp.int32, (C, LANES), 1)
    tloc = lax.broadcasted_iota(jnp.int32, (C, LANES), 0) & (GLA_SUB - 1)
    for slab in range(GLA_HEADS // 2):
        seg = slice(LANES * slab, LANES * (slab + 1))
        la = la_ref[:, seg]
        q = q_ref[:, seg]
        k = k_ref[:, seg]
        cum = jnp.dot(tri, la, preferred_element_type=F32, precision=hi)
        ref = jnp.dot(sub_start, la, preferred_element_type=F32, precision=hi)
        g_last = cum[C - 1:C, :]
        s_old = s_ref[slab]
        s_old_b = s_old.astype(BF16)
        q_state = q * jnp.exp(cum)
        q_off = q * jnp.exp(cum - ref)
        k_dec = k * jnp.exp(g_last - cum)
        diag = [q * k]
        for d in range(1, GLA_SUB):
            diag.append(jnp.where(tloc >= d,
                                  q * pltpu.roll(k, d, axis=0) * jnp.exp(cum - pltpu.roll(cum, d, axis=0)), 0.0))
        dec = jnp.exp(lax.dot_general(la, ones, (((0,), (0,)), ((), ())), preferred_element_type=F32, precision=hi))
        s_new = dec * s_old
        k_off = [(k * jnp.exp(jnp.minimum(ref[GLA_SUB * i:GLA_SUB * i + 1, :] - cum, 0.0))).astype(BF16)
                 for i in range(1, C // GLA_SUB)]
        for hh in range(2):
            head = 2 * slab + hh
            mine = (lane >= GLA_DK * hh) & (lane < GLA_DK * (hh + 1))
            v = v_ref[:, GLA_DV * head:GLA_DV * (head + 1)]
            vb = v.astype(BF16)
            o = _dot(jnp.where(mine, q_state, 0.0).astype(BF16), s_old_b)
            qt = jnp.where(mine, q_off, 0.0).astype(BF16)
            a_off = jnp.zeros((C, C), F32)
            for i in range(1, C // GLA_SUB):
                in_block = (row >= GLA_SUB * i) & (row < GLA_SUB * (i + 1)) & (col < GLA_SUB * i)
                a_off = jnp.where(in_block, _dot_nt(qt, k_off[i - 1]), a_off)
            o = o + _dot(a_off.astype(BF16), vb)
            for d in range(GLA_SUB):
                a_d = jnp.sum(jnp.where(mine, diag[d], 0.0), axis=1, keepdims=True)
                o = o + a_d * (v if d == 0 else pltpu.roll(v, d, axis=0))
            o_ref[:, GLA_DV * head:GLA_DV * (head + 1)] = o
            s_new = s_new + lax.dot_general(jnp.where(mine, k_dec, 0.0).astype(BF16), vb, (((0,), (0,)), ((), ())),
                                            preferred_element_type=F32)
        s_ref[slab] = s_new

    @pl.when(c == pl.num_programs(1) - 1)
    def _():
        sfin_ref[0] = s_ref[...]


def _gla_prompt(qg, kg, la, vg, batch):
    t = qg.shape[0]
    nc = t // batch // GLA_CHUNK
    n_slab = GLA_HEADS // 2
    row = lambda n: pl.BlockSpec((GLA_CHUNK, n), lambda b, c: (b * nc + c, 0))
    o, s_fin = pl.pallas_call(
        _gla_prompt_kernel,
        grid=(batch, nc),
        in_specs=[row(256), row(256), row(256), row(512)],
        out_specs=[row(512), pl.BlockSpec((1, n_slab, 2 * GLA_DK, GLA_DV), lambda b, c: (b, 0, 0, 0))],
        out_shape=[jax.ShapeDtypeStruct((t, GLA_V_W), F32),
                   jax.ShapeDtypeStruct((batch, n_slab, 2 * GLA_DK, GLA_DV), F32)],
        scratch_shapes=[pltpu.VMEM((n_slab, 2 * GLA_DK, GLA_DV), F32)],
        compiler_params=_cparams(("parallel", "arbitrary")),
        name="gla_prompt",
    )(qg, kg, la, vg)
    return o, s_fin.reshape(batch, GLA_HEADS, GLA_DK, GLA_DV)


GLA_DEC_BLOCK = 16


def _gla_sample_kernel(qt_ref, kt_ref, lat_ref, v_ref, s_ref, o_ref, snew_ref):
    for i in range(GLA_DEC_BLOCK):
        for h in range(GLA_HEADS):
            rows = slice(GLA_DK * h, GLA_DK * (h + 1))
            qc = qt_ref[0, rows, i:i + 1]
            kc = kt_ref[0, rows, i:i + 1]
            ac = jnp.exp(lat_ref[0, rows, i:i + 1])
            vr = v_ref[i:i + 1, GLA_DV * h:GLA_DV * (h + 1)]
            s_new = ac * s_ref[i, h] + kc * vr
            snew_ref[i, h] = s_new
            o_ref[i:i + 1, GLA_DV * h:GLA_DV * (h + 1)] = jnp.sum(qc * s_new, axis=0, keepdims=True)


def _gla_sample(qg, kg, la, vg, states, layer):
    n = qg.shape[0]
    nb = n // GLA_DEC_BLOCK
    cols = lambda a: a.reshape(nb, GLA_DEC_BLOCK, GLA_QK_W).transpose(0, 2, 1)
    colspec = pl.BlockSpec((1, GLA_QK_W, GLA_DEC_BLOCK), lambda i: (i, 0, 0))
    sblock = (GLA_DEC_BLOCK, GLA_HEADS, GLA_DK, GLA_DV)
    return pl.pallas_call(
        _gla_sample_kernel,
        grid=(nb,),
        in_specs=[colspec, colspec, colspec, pl.BlockSpec((GLA_DEC_BLOCK, GLA_V_W), lambda i: (i, 0)),
                  pl.BlockSpec((None,) + sblock, lambda i: (layer, i, 0, 0, 0))],
        out_specs=[pl.BlockSpec((GLA_DEC_BLOCK, GLA_V_W), lambda i: (i, 0)),
                   pl.BlockSpec(sblock, lambda i: (i, 0, 0, 0))],
        out_shape=[jax.ShapeDtypeStruct((n, GLA_V_W), F32), jax.ShapeDtypeStruct(states.shape[1:], F32)],
        compiler_params=_cparams(("parallel",)),
        name="gla_sample",
    )(cols(qg), cols(kg), cols(la), vg, states)


FLASH_TQ = 256
FLASH_TK = 1024


def _causal_flash(get_q, n_groups, qi, k_ref, get_v, m_ref, l_ref, acc_ref):
    tq, tk = FLASH_TQ, FLASH_TK
    m_ref[...] = jnp.full_like(m_ref, NEG_BIG)
    acc_ref[...] = jnp.zeros_like(acc_ref)
    if l_ref is not None:
        l_ref[...] = jnp.zeros_like(l_ref)

    def chunk(start, masked):
        k = k_ref[pl.ds(start, tk), :]
        v = get_v(k, start)
        if masked:
            tok = qi * tq + lax.broadcasted_iota(jnp.int32, (tq, tk), 0)
            key = start + lax.broadcasted_iota(jnp.int32, (tq, tk), 1)
            visible = key <= tok
        for g in range(n_groups):
            s = _dot_nt(get_q(g), k)
            if masked:
                s = jnp.where(visible, s, NEG_BIG)
            m_prev = m_ref[g]
            m_new = jnp.maximum(m_prev, jnp.max(s, axis=1, keepdims=True))
            alpha = jnp.exp2(m_prev - m_new)
            p = jnp.exp2(s - m_new)
            if l_ref is not None:
                l_ref[g] = alpha * l_ref[g] + jnp.sum(p, axis=1, keepdims=True)
            acc_ref[g] = alpha * acc_ref[g] + _dot(p.astype(BF16), v)
            m_ref[g] = m_new

    n_chunks = ((qi + 1) * tq + tk - 1) // tk

    def body(i, carry):
        chunk(pl.multiple_of(i * tk, tk), False)
        return carry

    lax.fori_loop(0, n_chunks - 1, body, 0)
    chunk(pl.multiple_of((n_chunks - 1) * tk, tk), True)


def _mla_prompt_kernel(q_ref, k_ref, o_ref, m_ref, l_ref, acc_ref):
    _causal_flash(lambda g: q_ref[g], MLA_HEADS, pl.program_id(1), k_ref, lambda k, start: k[:, :MLA_KV_RANK],
                  m_ref, l_ref, acc_ref)
    o_ref[...] = acc_ref[...] / l_ref[...]


def _mla_prompt(qcat, kcat, batch):
    t = kcat.shape[0]
    seq = t // batch
    tq = FLASH_TQ
    assert seq % FLASH_TK == 0
    nq = seq // tq
    return pl.pallas_call(
        _mla_prompt_kernel,
        grid=(batch, nq),
        in_specs=[pl.BlockSpec((MLA_HEADS, tq, MLA_QK_PAD), lambda b, i: (0, b * nq + i, 0)),
                  pl.BlockSpec((seq, MLA_QK_PAD), lambda b, i: (b, 0))],
        out_specs=pl.BlockSpec((MLA_HEADS, tq, MLA_KV_RANK), lambda b, i: (0, b * nq + i, 0)),
        out_shape=jax.ShapeDtypeStruct((MLA_HEADS, t, MLA_KV_RANK), F32),
        scratch_shapes=[pltpu.VMEM((MLA_HEADS, tq, 1), F32), pltpu.VMEM((MLA_HEADS, tq, 1), F32),
                        pltpu.VMEM((MLA_HEADS, tq, MLA_KV_RANK), F32)],
        compiler_params=_cparams(("parallel", "arbitrary")),
        name="mla_prompt",
    )(qcat, kcat)


MLA_DEC_PAGES = 64
DEC_ROWS = 8


def _mla_sample_kernel(pt_ref, q_ref, knew_ref, *refs, pp):
    ckv_refs = refs[:pp]
    krt_refs = refs[pp:2 * pp]
    o_ref, m_ref, l_ref, acc_ref = refs[2 * pp:]
    c = pl.program_id(1)

    @pl.when(c == 0)
    def _():
        m_ref[...] = jnp.full_like(m_ref, NEG_BIG)
        l_ref[...] = jnp.zeros_like(l_ref)
        acc_ref[...] = jnp.zeros_like(acc_ref)

    q = q_ref[0]
    q_lat = q[:, :MLA_KV_RANK]
    q_rope = q[:, MLA_KV_RANK:MLA_KV_RANK + MLA_ROPE_DIM]
    pages = [r[...].astype(BF16) for r in ckv_refs]
    s = jnp.concatenate(
        [_dot_nt(q_lat, pages[i]) + _dot(q_rope, krt_refs[i][...].astype(BF16)) for i in range(pp)], axis=1)
    m_prev = m_ref[...]
    m_new = jnp.maximum(m_prev, jnp.max(s, axis=1, keepdims=True))
    alpha = jnp.exp2(m_prev - m_new)
    p = jnp.exp2(s - m_new)
    l_ref[...] = alpha * l_ref[...] + jnp.sum(p, axis=1, keepdims=True)
    pb = p.astype(BF16)
    psz = pages[0].shape[0]
    pv = _dot(pb[:, :psz], pages[0])
    for i in range(1, pp):
        pv = pv + _dot(pb[:, psz * i:psz * (i + 1)], pages[i])
    acc_ref[...] = alpha * acc_ref[...] + pv
    m_ref[...] = m_new

    @pl.when(c == pl.num_programs(1) - 1)
    def _():
        knew = knew_ref[0]
        s_new = jnp.sum(q.astype(F32) * knew.astype(F32), axis=1, keepdims=True)
        m_old = m_ref[...]
        m_fin = jnp.maximum(m_old, s_new)
        a = jnp.exp2(m_old - m_fin)
        p_new = jnp.exp2(s_new - m_fin)
        l_fin = a * l_ref[...] + p_new
        v_new = knew[:, :MLA_KV_RANK].astype(F32)
        o_ref[0] = (a * acc_ref[...] + p_new.astype(BF16).astype(F32) * v_new) / l_fin


def _mla_sample(page_table, q, knew, cache_ckv, cache_krt, layer):
    n, n_pages = page_table.shape
    pp = min(MLA_DEC_PAGES, n_pages)
    assert n_pages % pp == 0
    page = cache_ckv.shape[2]

    def pool_spec(shape, i):
        return pl.BlockSpec((None, None) + shape, lambda b, c, pt: (layer, pt[b * n_pages + c * pp + i], 0, 0))

    grid_spec = pltpu.PrefetchScalarGridSpec(
        num_scalar_prefetch=1,
        grid=(n, n_pages // pp),
        in_specs=([pl.BlockSpec((1, DEC_ROWS, MLA_QK_PAD), lambda b, c, pt: (b, 0, 0)),
                   pl.BlockSpec((1, 1, MLA_QK_PAD), lambda b, c, pt: (b, 0, 0))]
                  + [pool_spec((page, MLA_KV_RANK), i) for i in range(pp)]
                  + [pool_spec((MLA_ROPE_DIM, page), i) for i in range(pp)]),
        out_specs=pl.BlockSpec((1, DEC_ROWS, MLA_KV_RANK), lambda b, c, pt: (b, 0, 0)),
        scratch_shapes=[pltpu.VMEM((DEC_ROWS, 1), F32), pltpu.VMEM((DEC_ROWS, 1), F32),
                        pltpu.VMEM((DEC_ROWS, MLA_KV_RANK), F32)],
    )
    return pl.pallas_call(
        functools.partial(_mla_sample_kernel, pp=pp),
        grid_spec=grid_spec,
        out_shape=jax.ShapeDtypeStruct((n, DEC_ROWS, MLA_KV_RANK), F32),
        compiler_params=_cparams(("parallel", "arbitrary")),
        name="mla_sample",
    )(page_table.reshape(-1), q, knew, *([cache_ckv] * pp), *([cache_krt] * pp))


def _even_mix_kernel(y_ref, og_ref, rg_ref, olat_ref, glanorm_ref, wuv_ref, wout_ref, gpost_ref, out_ref):
    pieces = []
    for h in range(GLA_HEADS):
        seg = slice(GLA_DV * h, GLA_DV * (h + 1))
        pieces.append((_rms(og_ref[:, seg], glanorm_ref[...]) * _silu(rg_ref[:, seg])).astype(BF16))
    for h in range(MLA_HEADS):
        pieces.append(_dot(olat_ref[h].astype(BF16), wuv_ref[h]).astype(BF16))
    mix = jnp.concatenate(pieces, axis=1)
    out_ref[...] = y_ref[...] + _rms(_dot(mix, wout_ref[...]), gpost_ref[...])


def _even_mix(y, og, rg, olat, w, tm):
    t = y.shape[0]
    row = lambda n: pl.BlockSpec((tm, n), lambda i: (i, 0))
    full2 = lambda a: pl.BlockSpec(a.shape, lambda i: (0, 0))
    full3 = lambda a: pl.BlockSpec(a.shape, lambda i: (0, 0, 0))
    return pl.pallas_call(
        _even_mix_kernel,
        grid=(t // tm,),
        in_specs=[row(D_MODEL), row(GLA_V_W), row(GLA_V_W),
                  pl.BlockSpec((MLA_HEADS, tm, MLA_KV_RANK), lambda i: (0, i, 0)),
                  full2(w["gla_norm"]), full3(w["w_uv"]), full2(w["w_out"]), full2(w["g_post"])],
        out_specs=row(D_MODEL),
        out_shape=jax.ShapeDtypeStruct((t, D_MODEL), F32),
        compiler_params=_cparams(("parallel",)),
        name="even_mix",
    )(y, og, rg, olat, w["gla_norm"], w["w_uv"], w["w_out"], w["g_post"])


def _odd_proj_kernel(y_ref, cos_ref, sin_ref, gpre_ref, win_ref, dk_ref, dv_ref, qd_ref, kd_ref, vd_ref):
    h = _rms(y_ref[...], gpre_ref[...]).astype(BF16)
    p = _dot(h, win_ref[...])
    cos = cos_ref[...]
    sin = sin_ref[...]
    lane = lax.broadcasted_iota(jnp.int32, cos.shape, 1)
    low = lane < DIFF_DH
    sm = DIFF_DH ** -0.5 * LOG2E
    for head in range(DIFF_HEADS):
        g, r = divmod(head, DIFF_REP)
        slab = _rope128(p[:, LANES * head:LANES * (head + 1)], cos, sin) * sm
        qd_ref[g, r] = jnp.where(low, slab, 0.0).astype(BF16)
        qd_ref[g, DIFF_REP + r] = jnp.where(low, 0.0, slab).astype(BF16)
    for g in range(DIFF_KV_HEADS):
        seg = slice(LANES * g, LANES * (g + 1))
        kk = _rope128(p[:, DIFF_Q_W + LANES * g:DIFF_Q_W + LANES * (g + 1)], cos, sin)
        dk_ref[:, seg] = kk
        kd_ref[:, seg] = kk.astype(BF16)
    vv = p[:, DIFF_Q_W + DIFF_K_W:ODD_IN]
    dv_ref[...] = vv
    vd_ref[...] = vv.astype(BF16)


def _odd_proj(y, cos, sin, w, tm):
    t = y.shape[0]
    row = lambda n: pl.BlockSpec((tm, n), lambda i: (i, 0))
    full2 = lambda a: pl.BlockSpec(a.shape, lambda i: (0, 0))
    return pl.pallas_call(
        _odd_proj_kernel,
        grid=(t // tm,),
        in_specs=[row(D_MODEL), _table_spec(cos, tm), _table_spec(cos, tm), full2(w["g_pre"]), full2(w["w_in"])],
        out_specs=[row(DIFF_K_W), row(DIFF_V_W),
                   pl.BlockSpec((DIFF_KV_HEADS, 2 * DIFF_REP, tm, LANES), lambda i: (0, 0, i, 0)),
                   row(DIFF_K_W), row(DIFF_V_W)],
        out_shape=[jax.ShapeDtypeStruct((t, DIFF_K_W), F32), jax.ShapeDtypeStruct((t, DIFF_V_W), F32),
                   jax.ShapeDtypeStruct((DIFF_KV_HEADS, 2 * DIFF_REP, t, LANES), BF16),
                   jax.ShapeDtypeStruct((t, DIFF_K_W), BF16), jax.ShapeDtypeStruct((t, DIFF_V_W), BF16)],
        compiler_params=_cparams(("parallel",)),
        name="odd_proj",
    )(y, cos, sin, w["g_pre"], w["w_in"])


def _diff_lambda(lam_ref, lam_init):
    lq1, lk1, lq2, lk2 = lam_ref[0:1, :], lam_ref[1:2, :], lam_ref[2:3, :], lam_ref[3:4, :]
    return (jnp.exp(jnp.sum(lq1 * lk1, axis=1, keepdims=True))
            - jnp.exp(jnp.sum(lq2 * lk2, axis=1, keepdims=True)) + lam_init)


def _diff_prompt_kernel(lam_ref, q_ref, k_ref, v_ref, o_ref, m_ref, acc_ref, *, lam_init):
    def values(k, start):
        v = v_ref[pl.ds(start, FLASH_TK), :]
        return jnp.concatenate([v, jnp.ones_like(v)], axis=1)

    _causal_flash(lambda g: q_ref[0, g], 2 * DIFF_REP, pl.program_id(2), k_ref, values, m_ref, None, acc_ref)
    lam = _diff_lambda(lam_ref, lam_init)
    for r in range(DIFF_REP):
        a0 = acc_ref[r]
        a1 = acc_ref[DIFF_REP + r]
        o_ref[:, DIFF_DV * r:DIFF_DV * (r + 1)] = (
            a0[:, :DIFF_DV] / a0[:, DIFF_DV:] - lam * (a1[:, :DIFF_DV] / a1[:, DIFF_DV:]))


def _diff_prompt(qd, kd, vd, lam_vecs, lam_init, batch):
    t = kd.shape[0]
    seq = t // batch
    tq = FLASH_TQ
    assert seq % FLASH_TK == 0
    nq = seq // tq
    groups = 2 * DIFF_REP
    return pl.pallas_call(
        functools.partial(_diff_prompt_kernel, lam_init=lam_init),
        grid=(batch, DIFF_KV_HEADS, nq),
        in_specs=[pl.BlockSpec(lam_vecs.shape, lambda b, g, i: (0, 0)),
                  pl.BlockSpec((1, 2 * DIFF_REP, tq, LANES), lambda b, g, i: (g, 0, b * nq + i, 0)),
                  pl.BlockSpec((seq, LANES), lambda b, g, i: (b, g)),
                  pl.BlockSpec((seq, DIFF_DV), lambda b, g, i: (b, g))],
        out_specs=pl.BlockSpec((tq, DIFF_REP * DIFF_DV), lambda b, g, i: (b * nq + i, g)),
        out_shape=jax.ShapeDtypeStruct((t, DIFF_HEADS * DIFF_DV), F32),
        scratch_shapes=[pltpu.VMEM((groups, tq, 1), F32), pltpu.VMEM((groups, tq, 2 * DIFF_DV), F32)],
        compiler_params=_cparams(("parallel", "parallel", "arbitrary")),
        name="diff_prompt",
    )(lam_vecs, qd, kd, vd)


DIFF_DEC_ROWS = 2 * DIFF_HEADS
DIFF_DEC_PAGES = 32


def _diff_sample_kernel(pt_ref, lam_ref, q_ref, knew_ref, vnew_ref, *refs, lam_init, pp):
    k_refs = refs[:pp]
    v_refs = refs[pp:2 * pp]
    o_ref, m_ref, l_ref, acc_ref = refs[2 * pp:]
    c = pl.program_id(1)

    @pl.when(c == 0)
    def _():
        m_ref[...] = jnp.full_like(m_ref, NEG_BIG)
        l_ref[...] = jnp.zeros_like(l_ref)
        acc_ref[...] = jnp.zeros_like(acc_ref)

    q = q_ref[0]
    s = jnp.concatenate([_dot_nt(q, k_refs[i][...].astype(BF16)) for i in range(pp)], axis=1)
    row_head = lax.broadcasted_iota(jnp.int32, s.shape, 0) // (2 * DIFF_REP)
    col_head = lax.broadcasted_iota(jnp.int32, s.shape, 1) & (DIFF_KV_HEADS - 1)
    s = jnp.where(row_head == col_head, s, NEG_BIG)
    m_prev = m_ref[...]
    m_new = jnp.maximum(m_prev, jnp.max(s, axis=1, keepdims=True))
    alpha = jnp.exp2(m_prev - m_new)
    p = jnp.exp2(s - m_new)
    l_ref[...] = alpha * l_ref[...] + jnp.sum(p, axis=1, keepdims=True)
    pb = p.astype(BF16)
    psz = k_refs[0].shape[0]
    pv = _dot(pb[:, :psz], v_refs[0][...].astype(BF16))
    for i in range(1, pp):
        pv = pv + _dot(pb[:, psz * i:psz * (i + 1)], v_refs[i][...].astype(BF16))
    acc_ref[...] = alpha * acc_ref[...] + pv
    m_ref[...] = m_new

    @pl.when(c == pl.num_programs(1) - 1)
    def _():
        first = lax.broadcasted_iota(jnp.int32, (DIFF_DEC_ROWS, LANES), 0) < 2 * DIFF_REP
        knew = knew_ref[0].astype(F32)
        vnew = vnew_ref[0].astype(F32)
        k_row = jnp.where(first, knew[:, :LANES], knew[:, LANES:])
        v_row = jnp.where(first, vnew[:, :DIFF_DV], vnew[:, DIFF_DV:])
        s_new = jnp.sum(q.astype(F32) * k_row, axis=1, keepdims=True)
        m_old = m_ref[...]
        m_fin = jnp.maximum(m_old, s_new)
        a = jnp.exp2(m_old - m_fin)
        p_new = jnp.exp2(s_new - m_fin)
        l_fin = a * l_ref[...] + p_new
        o = (a * acc_ref[...] + p_new.astype(BF16).astype(F32) * v_row) / l_fin
        lam = _diff_lambda(lam_ref, lam_init)
        for g in range(DIFF_KV_HEADS):
            base = 2 * DIFF_REP * g
            o_ref[0, DIFF_REP * g:DIFF_REP * (g + 1), :] = (
                o[base:base + DIFF_REP] - lam * o[base + DIFF_REP:base + 2 * DIFF_REP])


def _diff_sample(page_table, lam_vecs, lam_init, q, knew, vnew, cache_k, cache_v, layer):
    n, n_pages = page_table.shape
    pp = min(DIFF_DEC_PAGES, n_pages)
    assert n_pages % pp == 0
    page_rows = cache_k.shape[2]

    def pool_spec(i):
        return pl.BlockSpec((None, None, page_rows, LANES),
                            lambda b, c, pt: (layer, pt[b * n_pages + c * pp + i], 0, 0))

    grid_spec = pltpu.PrefetchScalarGridSpec(
        num_scalar_prefetch=1,
        grid=(n, n_pages // pp),
        in_specs=([pl.BlockSpec(lam_vecs.shape, lambda b, c, pt: (0, 0)),
                   pl.BlockSpec((1, DIFF_DEC_ROWS, LANES), lambda b, c, pt: (b, 0, 0)),
                   pl.BlockSpec((1, 1, DIFF_K_W), lambda b, c, pt: (b, 0, 0)),
                   pl.BlockSpec((1, 1, DIFF_V_W), lambda b, c, pt: (b, 0, 0))]
                  + [pool_spec(i) for i in range(pp)] + [pool_spec(i) for i in range(pp)]),
        out_specs=pl.BlockSpec((1, DIFF_HEADS, DIFF_DV), lambda b, c, pt: (b, 0, 0)),
        scratch_shapes=[pltpu.VMEM((DIFF_DEC_ROWS, 1), F32), pltpu.VMEM((DIFF_DEC_ROWS, 1), F32),
                        pltpu.VMEM((DIFF_DEC_ROWS, DIFF_DV), F32)],
    )
    return pl.pallas_call(
        functools.partial(_diff_sample_kernel, lam_init=lam_init, pp=pp),
        grid_spec=grid_spec,
        out_shape=jax.ShapeDtypeStruct((n, DIFF_HEADS, DIFF_DV), F32),
        compiler_params=_cparams(("parallel", "arbitrary")),
        name="diff_sample",
    )(page_table.reshape(-1), lam_vecs, q, knew, vnew, *([cache_k] * pp), *([cache_v] * pp))


def _odd_mix_kernel(y_ref, o_ref, subln_ref, wout_ref, gpost_ref, out_ref, *, lam_init):
    pieces = []
    for h in range(DIFF_HEADS):
        seg = slice(DIFF_DV * h, DIFF_DV * (h + 1))
        pieces.append((_rms(o_ref[:, seg], subln_ref[...]) * (1.0 - lam_init)).astype(BF16))
    mix = jnp.concatenate(pieces, axis=1)
    out_ref[...] = y_ref[...] + _rms(_dot(mix, wout_ref[...]), gpost_ref[...])


def _odd_mix(y, o, w, lam_init, tm):
    t = y.shape[0]
    row = lambda n: pl.BlockSpec((tm, n), lambda i: (i, 0))
    full2 = lambda a: pl.BlockSpec(a.shape, lambda i: (0, 0))
    return pl.pallas_call(
        functools.partial(_odd_mix_kernel, lam_init=lam_init),
        grid=(t // tm,),
        in_specs=[row(D_MODEL), row(D_MODEL), full2(w["subln"]), full2(w["w_out"]), full2(w["g_post"])],
        out_specs=row(D_MODEL),
        out_shape=jax.ShapeDtypeStruct((t, D_MODEL), F32),
        compiler_params=_cparams(("parallel",)),
        name="odd_mix",
    )(y, o, w["subln"], w["w_out"], w["g_post"])


FFN_SPLIT = 2


def _ffn_kernel(y_ref, gpre_ref, wg_ref, wu_ref, wd_ref, gpost_ref, out_ref, h_ref, acc_ref):
    h_ref[...] = _rms(y_ref[...], gpre_ref[...]).astype(BF16)
    h = h_ref[...]
    w = wg_ref.shape[1] // FFN_SPLIT
    for s in range(FFN_SPLIT):
        cols = slice(w * s, w * (s + 1))
        a = (_silu(_dot(h, wg_ref[:, cols])) * _dot(h, wu_ref[:, cols])).astype(BF16)
        part = _dot(a, wd_ref[cols, :])
        if s == 0:
            acc_ref[...] = part
        else:
            acc_ref[...] += part
    out_ref[...] = y_ref[...] + _rms(acc_ref[...], gpost_ref[...])


def _ffn(y, w, tm):
    t = y.shape[0]
    d_ff = w["w_gate"].shape[1]
    assert d_ff % (FFN_SPLIT * LANES) == 0
    const = lambda i: (0, 0)
    return pl.pallas_call(
        _ffn_kernel,
        grid=(t // tm,),
        in_specs=[pl.BlockSpec((tm, D_MODEL), lambda i: (i, 0)),
                  pl.BlockSpec((1, D_MODEL), const),
                  pl.BlockSpec((D_MODEL, d_ff), const),
                  pl.BlockSpec((D_MODEL, d_ff), const),
                  pl.BlockSpec((d_ff, D_MODEL), const),
                  pl.BlockSpec((1, D_MODEL), const)],
        out_specs=pl.BlockSpec((tm, D_MODEL), lambda i: (i, 0)),
        out_shape=jax.ShapeDtypeStruct((t, D_MODEL), F32),
        scratch_shapes=[pltpu.VMEM((tm, D_MODEL), BF16), pltpu.VMEM((tm, D_MODEL), F32)],
        compiler_params=_cparams(("parallel",)),
        name="ffn",
    )(y, w["g_pre"], w["w_gate"], w["w_up"], w["w_down"], w["g_post"])


def _rope_tables(pos):
    half = MLA_ROPE_DIM // 2
    inv = ROPE_THETA ** (-jnp.arange(0, MLA_ROPE_DIM, 2, dtype=F32) / MLA_ROPE_DIM)
    ang = pos.astype(F32)[:, None] * inv[None, :]
    cos, sin = jnp.cos(ang), jnp.sin(ang)
    assert cos.shape[1] == half
    return jnp.tile(cos, (1, 4)), jnp.tile(jnp.concatenate([-sin, sin], axis=1), (1, 2))


def _row(v):
    return v.reshape(1, -1).astype(F32)


def _pack_even(j, norm_mix_pre_i, norm_mix_post_i, w_in_even, w_gla_alpha, b_gla_alpha, gla_out_norm, mla_q_norm,
               w_mla_uq, mla_kv_norm, w_mla_uk, w_mla_uv, w_out_even):
    w_in = w_in_even[j]
    widths = [GLA_QK_W, GLA_QK_W, GLA_V_W, GLA_GATE_RANK, GLA_V_W, MLA_Q_RANK, MLA_KV_RANK, MLA_ROPE_DIM]
    offs = np.concatenate([[0], np.cumsum(widths)])
    seg = lambda n: w_in[:, int(offs[n]):int(offs[n + 1])]
    pad = jnp.zeros((D_MODEL, EVEN_IN_PAD - int(offs[-1])), w_in.dtype)
    packed = jnp.concatenate([seg(0), seg(1), seg(2), seg(4), seg(5), seg(6), seg(7), seg(3), pad], axis=1)
    w_alpha = jnp.zeros((LANES, GLA_QK_W), F32).at[MLA_ROPE_DIM:MLA_ROPE_DIM + GLA_GATE_RANK].set(w_gla_alpha[j])
    uq = w_mla_uq[j].reshape(MLA_Q_RANK, MLA_HEADS, MLA_NOPE_DIM + MLA_ROPE_DIM)
    uq = jnp.concatenate([uq[:, :, :MLA_NOPE_DIM].reshape(MLA_Q_RANK, -1),
                          uq[:, :, MLA_NOPE_DIM:].reshape(MLA_Q_RANK, -1)], axis=1)
    return {
        "g_pre": _row(norm_mix_pre_i), "g_post": _row(norm_mix_post_i),
        "w_in": packed.astype(BF16), "w_alpha": w_alpha.astype(BF16), "b_alpha": _row(b_gla_alpha[j]),
        "q_norm": _row(mla_q_norm[j]), "kv_norm": _row(mla_kv_norm[j]), "w_uq": uq.astype(BF16),
        "w_ukt": jnp.transpose(w_mla_uk[j], (1, 2, 0)).astype(BF16),
        "w_uv": jnp.transpose(w_mla_uv[j], (1, 0, 2)).astype(BF16),
        "gla_norm": _row(gla_out_norm[j]), "w_out": w_out_even[j].astype(BF16),
    }


def _decode_q_mla(qcat):
    q = jnp.transpose(qcat, (1, 0, 2))
    return jnp.pad(q, ((0, 0), (0, DEC_ROWS - MLA_HEADS), (0, 0)))


def _decode_q_diff(qd):
    n = qd.shape[2]
    return jnp.transpose(qd, (2, 0, 1, 3)).reshape(n, DIFF_DEC_ROWS, LANES)


def kernel(x_prompt, x_sample, state_gla, cache_mla_ckv, cache_mla_krope, cache_diff_k, cache_diff_v, page_table,
           norm_mix_pre, norm_mix_post, norm_ffn_pre, norm_ffn_post,
           w_in_even, w_gla_alpha, b_gla_alpha, gla_out_norm, mla_q_norm, w_mla_uq, mla_kv_norm, w_mla_uk, w_mla_uv,
           w_out_even,
           w_in_odd, diff_lambda_q1, diff_lambda_k1, diff_lambda_q2, diff_lambda_k2, diff_subln, w_out_odd,
           w_ffn_gate, w_ffn_up, w_ffn_down):
    batch, seq, _ = x_prompt.shape
    n_dec, dec_seq, _ = x_sample.shape
    assert dec_seq == 1
    depth = norm_mix_pre.shape[0]
    n_pages = page_table.shape[1]
    page = cache_mla_ckv.shape[2]
    past_len = n_pages * page
    tp = batch * seq
    tm_p = min(512, seq)
    tm_s = n_dec

    cos_p, sin_p = _rope_tables(jnp.arange(seq, dtype=jnp.int32))
    cos_s, sin_s = _rope_tables(jnp.full((n_dec,), past_len, jnp.int32))

    cache_krt = jnp.transpose(cache_mla_krope, (0, 1, 3, 2))
    cache_dk = cache_diff_k.reshape(cache_diff_k.shape[:2] + (page * DIFF_KV_HEADS, 2 * DIFF_DH))
    cache_dv = cache_diff_v.reshape(cache_diff_v.shape[:2] + (page * DIFF_KV_HEADS, DIFF_DV))

    yp = x_prompt.reshape(tp, D_MODEL)
    ys = x_sample.reshape(n_dec, D_MODEL)
    gla_p, gla_s, ckv_p, kr_p, ckv_s, kr_s = [], [], [], [], [], []
    dk_p, dv_p, dk_s, dv_s = [], [], [], []
    for i in range(depth):
        j = i // 2
        if i % 2 == 0:
            w = _pack_even(j, norm_mix_pre[i], norm_mix_post[i], w_in_even, w_gla_alpha, b_gla_alpha, gla_out_norm,
                           mla_q_norm, w_mla_uq, mla_kv_norm, w_mla_uk, w_mla_uv, w_out_even)
            qg, kg, vg, rg, la, ckv, krope, qcat, kcat = _even_proj(yp, cos_p, sin_p, w, tm_p)
            og, s_fin = _gla_prompt(qg, kg, la, vg, batch)
            olat = _mla_prompt(qcat, kcat, batch)
            yp = _even_mix(yp, og, rg, olat, w, tm_p)
            gla_p.append(s_fin)
            ckv_p.append(ckv.reshape(batch, seq, MLA_KV_RANK))
            kr_p.append(krope.reshape(batch, seq, MLA_ROPE_DIM))

            qg, kg, vg, rg, la, ckv, krope, qcat, kcat = _even_proj(ys, cos_s, sin_s, w, tm_s)
            og, s_new = _gla_sample(qg, kg, la, vg, state_gla, j)
            olat = _mla_sample(page_table, _decode_q_mla(qcat), kcat.reshape(n_dec, 1, MLA_QK_PAD),
                               cache_mla_ckv, cache_krt, j)
            olat = jnp.transpose(olat[:, :MLA_HEADS], (1, 0, 2))
            ys = _even_mix(ys, og, rg, olat, w, tm_s)
            gla_s.append(s_new)
            ckv_s.append(ckv.reshape(n_dec, 1, MLA_KV_RANK))
            kr_s.append(krope.reshape(n_dec, 1, MLA_ROPE_DIM))
        else:
            lam_init = 0.8 - 0.6 * math.exp(-0.3 * i)
            w = {"g_pre": _row(norm_mix_pre[i]), "g_post": _row(norm_mix_post[i]), "w_in": w_in_odd[j].astype(BF16),
                 "subln": _row(diff_subln[j]), "w_out": w_out_odd[j].astype(BF16)}
            lam_vecs = jnp.stack([diff_lambda_q1[j], diff_lambda_k1[j], diff_lambda_q2[j], diff_lambda_k2[j]]).astype(F32)
            dk, dv, qd, kd, vd = _odd_proj(yp, cos_p, sin_p, w, tm_p)
            o = _diff_prompt(qd, kd, vd, lam_vecs, lam_init, batch)
            yp = _odd_mix(yp, o, w, lam_init, tm_p)
            dk_p.append(dk.reshape(batch, seq, DIFF_KV_HEADS, 2 * DIFF_DH))
            dv_p.append(dv.reshape(batch, seq, DIFF_KV_HEADS, DIFF_DV))

            dk, dv, qd, kd, vd = _odd_proj(ys, cos_s, sin_s, w, tm_s)
            o = _diff_sample(page_table, lam_vecs, lam_init, _decode_q_diff(qd), kd.reshape(n_dec, 1, DIFF_K_W),
                             vd.reshape(n_dec, 1, DIFF_V_W), cache_dk, cache_dv, j)
            ys = _odd_mix(ys, o.reshape(n_dec, DIFF_HEADS * DIFF_DV), w, lam_init, tm_s)
            dk_s.append(dk.reshape(n_dec, 1, DIFF_KV_HEADS, 2 * DIFF_DH))
            dv_s.append(dv.reshape(n_dec, 1, DIFF_KV_HEADS, DIFF_DV))
        wf = {"g_pre": _row(norm_ffn_pre[i]), "g_post": _row(norm_ffn_post[i]), "w_gate": w_ffn_gate[i].astype(BF16),
              "w_up": w_ffn_up[i].astype(BF16), "w_down": w_ffn_down[i].astype(BF16)}
        yp = _ffn(yp, wf, tm_p)
        ys = _ffn(ys, wf, tm_s)
    return (yp.reshape(batch, seq, D_MODEL), ys.reshape(n_dec, 1, D_MODEL),
            jnp.stack(gla_p), jnp.stack(gla_s),
            jnp.stack(ckv_p), jnp.stack(kr_p), jnp.stack(ckv_s), jnp.stack(kr_s),
            jnp.stack(dk_p), jnp.stack(dv_p), jnp.stack(dk_s), jnp.stack(dv_s))
```

```python
import functools
import math

import jax
import jax.numpy as jnp
import numpy as np
from jax import lax
from jax.experimental import pallas as pl
from jax.experimental.pallas import tpu as pltpu

F32 = jnp.float32
BF16 = jnp.bfloat16

D_MODEL = 1024
EPS = 1e-6
ROPE_THETA = 10000.0
LANES = 128

GLA_HEADS = 4
GLA_DK = 64
GLA_DV = 128
GLA_QK_W = GLA_HEADS * GLA_DK
GLA_V_W = GLA_HEADS * GLA_DV
GLA_GATE_RANK = 16
GLA_GATE_TEMP = 16.0
GLA_CHUNK = 128
GLA_SUB = 16

MLA_HEADS = 4
MLA_Q_RANK = 384
MLA_KV_RANK = 256
MLA_NOPE_DIM = 128
MLA_ROPE_DIM = 64
MLA_V_DIM = 128
MLA_QK_PAD = MLA_KV_RANK + LANES

DIFF_HEADS = 8
DIFF_KV_HEADS = 2
DIFF_REP = DIFF_HEADS // DIFF_KV_HEADS
DIFF_DH = 64
DIFF_DV = 2 * DIFF_DH
DIFF_Q_W = DIFF_HEADS * 2 * DIFF_DH
DIFF_K_W = DIFF_KV_HEADS * 2 * DIFF_DH
DIFF_V_W = DIFF_KV_HEADS * DIFF_DV

EVEN_IN_PAD = 2304
ODD_IN = DIFF_Q_W + DIFF_K_W + DIFF_V_W
LOG2E = 1.4426950408889634
NEG_BIG = -1e30

VMEM_LIMIT = 56 * 1024 * 1024


def _cparams(sem):
    return pltpu.CompilerParams(dimension_semantics=sem, vmem_limit_bytes=VMEM_LIMIT)


def _rms(x, g):
    return x * lax.rsqrt(jnp.mean(x * x, axis=-1, keepdims=True) + EPS) * g


def _silu(x):
    return x * (1.0 / (1.0 + jnp.exp(-x)))


def _swap_halves64(x):
    lane = lax.broadcasted_iota(jnp.int32, x.shape, 1)
    return jnp.where((lane & 63) < 32, pltpu.roll(x, 96, axis=1), pltpu.roll(x, 32, axis=1))


def _rope128(x, cos, sin):
    return x * cos + _swap_halves64(x) * sin


def _table_spec(table, tm):
    n_tiles = table.shape[0] // tm
    return pl.BlockSpec((tm, LANES), lambda i: (i % n_tiles, 0))


def _dot(a, b):
    return jnp.dot(a, b, preferred_element_type=F32)


def _dot_nt(a, b):
    return lax.dot_general(a, b, (((1,), (1,)), ((), ())), preferred_element_type=F32)


def _even_proj_kernel(y_ref, cos_ref, sin_ref, gpre_ref, win_ref, walpha_ref, balpha_ref, qnorm_ref, kvnorm_ref,
                      wuq_ref, wukt_ref,
                      qg_ref, kg_ref, vg_ref, rg_ref, la_ref, ckv_ref, krope_ref, qcat_ref, kcat_ref):
    h = _rms(y_ref[...], gpre_ref[...]).astype(BF16)
    p = _dot(h, win_ref[...])
    qg_ref[...] = p[:, 0:256] * (GLA_DK ** -0.5)
    kg_ref[...] = p[:, 256:512]
    vg_ref[...] = p[:, 512:1024]
    rg_ref[...] = p[:, 1024:1536]
    cos = cos_ref[...]
    sin = sin_ref[...]
    lane = lax.broadcasted_iota(jnp.int32, cos.shape, 1)
    low = lane < MLA_ROPE_DIM

    tail = p[:, 2176:2304]
    z = _dot(tail.astype(BF16), walpha_ref[...]) + balpha_ref[...]
    la_ref[...] = (jnp.minimum(z, 0.0) - jnp.log(1.0 + jnp.exp(-jnp.abs(z)))) * (1.0 / GLA_GATE_TEMP)
    kr = _rope128(tail, cos, sin)
    krope_ref[...] = kr[:, :MLA_ROPE_DIM]

    ckv = _rms(p[:, 1920:2176], kvnorm_ref[...])
    ckv_ref[...] = ckv
    kcat_ref[:, 0:MLA_KV_RANK] = ckv.astype(BF16)
    kcat_ref[:, MLA_KV_RANK:MLA_QK_PAD] = jnp.where(low, kr, 0.0).astype(BF16)

    cq = _rms(p[:, 1536:1920], qnorm_ref[...]).astype(BF16)
    q = _dot(cq, wuq_ref[...])
    sm = (MLA_NOPE_DIM + MLA_ROPE_DIM) ** -0.5 * LOG2E
    for s in range(MLA_HEADS // 2):
        slab = _rope128(q[:, 512 + 128 * s:640 + 128 * s], cos, sin) * sm
        for hh in range(2):
            head = 2 * s + hh
            q_lat = _dot(q[:, 128 * head:128 * head + 128].astype(BF16), wukt_ref[head]) * sm
            qcat_ref[head, :, 0:MLA_KV_RANK] = q_lat.astype(BF16)
            rp = slab if hh == 0 else pltpu.roll(slab, 64, axis=1)
            qcat_ref[head, :, MLA_KV_RANK:MLA_QK_PAD] = jnp.where(low, rp, 0.0).astype(BF16)


def _even_proj(y, cos, sin, w, tm):
    t = y.shape[0]
    grid = (t // tm,)
    row = lambda n: pl.BlockSpec((tm, n), lambda i: (i, 0))
    full2 = lambda a: pl.BlockSpec(a.shape, lambda i: (0, 0))
    full3 = lambda a: pl.BlockSpec(a.shape, lambda i: (0, 0, 0))
    tab = _table_spec(cos, tm)
    outs = [
        jax.ShapeDtypeStruct((t, 256), F32), jax.ShapeDtypeStruct((t, 256), F32),
        jax.ShapeDtypeStruct((t, 512), F32), jax.ShapeDtypeStruct((t, 512), F32),
        jax.ShapeDtypeStruct((t, 256), F32),
        jax.ShapeDtypeStruct((t, MLA_KV_RANK), F32), jax.ShapeDtypeStruct((t, MLA_ROPE_DIM), F32),
        jax.ShapeDtypeStruct((MLA_HEADS, t, MLA_QK_PAD), BF16), jax.ShapeDtypeStruct((t, MLA_QK_PAD), BF16),
    ]
    out_specs = [row(256), row(256), row(512), row(512), row(256), row(MLA_KV_RANK), row(MLA_ROPE_DIM),
                 pl.BlockSpec((MLA_HEADS, tm, MLA_QK_PAD), lambda i: (0, i, 0)), row(MLA_QK_PAD)]
    return pl.pallas_call(
        _even_proj_kernel,
        grid=grid,
        in_specs=[row(D_MODEL), tab, tab, full2(w["g_pre"]), full2(w["w_in"]), full2(w["w_alpha"]),
                  full2(w["b_alpha"]), full2(w["q_norm"]), full2(w["kv_norm"]), full2(w["w_uq"]), full3(w["w_ukt"])],
        out_specs=out_specs,
        out_shape=outs,
        compiler_params=_cparams(("parallel",)),
        name="even_proj",
    )(y, cos, sin, w["g_pre"], w["w_in"], w["w_alpha"], w["b_alpha"], w["q_norm"], w["kv_norm"], w["w_uq"], w["w_ukt"])


def _gla_prompt_kernel(q_ref, k_ref, la_ref, v_ref, o_ref, sfin_ref, s_ref):
    c = pl.program_id(1)
    C = GLA_CHUNK
    hi = lax.Precision.HIGHEST

    @pl.when(c == 0)
    def _():
        s_ref[...] = jnp.zeros_like(s_ref)

    row = lax.broadcasted_iota(jnp.int32, (C, C), 0)
    col = lax.broadcasted_iota(jnp.int32, (C, C), 1)
    tri = (col <= row).astype(F32)
    sub_start = (col < (row & -GLA_SUB)).astype(F32)
    ones = jnp.ones((C, LANES), F32)
    lane = lax.broadcasted_iota(jnp.int32, (C, LANES), 1)
    tloc = lax.broadcasted_iota(jnp.int32, (C, LANES), 0) & (GLA_SUB - 1)
    for slab in range(GLA_HEADS // 2):
        seg = slice(LANES * slab, LANES * (slab + 1))
        la = la_ref[:, seg]
        q = q_ref[:, seg]
        k = k_ref[:, seg]
        cum = jnp.dot(tri, la, preferred_element_type=F32, precision=hi)
        ref = jnp.dot(sub_start, la, preferred_element_type=F32, precision=hi)
        g_last = cum[C - 1:C, :]
        s_old = s_ref[slab]
        s_old_b = s_old.astype(BF16)
        q_state = q * jnp.exp(cum)
        q_off = q * jnp.exp(cum - ref)
        k_dec = k * jnp.exp(g_last - cum)
        diag = [q * k]
        for d in range(1, GLA_SUB):
            diag.append(jnp.where(tloc >= d,
                                  q * pltpu.roll(k, d, axis=0) * jnp.exp(cum - pltpu.roll(cum, d, axis=0)), 0.0))
        dec = jnp.exp(lax.dot_general(la, ones, (((0,), (0,)), ((), ())), preferred_element_type=F32, precision=hi))
        s_new = dec * s_old
        k_off = [(k * jnp.exp(jnp.minimum(ref[GLA_SUB * i:GLA_SUB * i + 1, :] - cum, 0.0))).astype(BF16)
                 for i in range(1, C // GLA_SUB)]
        for hh in range(2):
            head = 2 * slab + hh
            mine = (lane >= GLA_DK * hh) & (lane < GLA_DK * (hh + 1))
            v = v_ref[:, GLA_DV * head:GLA_DV * (head + 1)]
            vb = v.astype(BF16)
            o = _dot(jnp.where(mine, q_state, 0.0).astype(BF16), s_old_b)
            qt = jnp.where(mine, q_off, 0.0).astype(BF16)
            a_off = jnp.zeros((C, C), F32)
            for i in range(1, C // GLA_SUB):
                in_block = (row >= GLA_SUB * i) & (row < GLA_SUB * (i + 1)) & (col < GLA_SUB * i)
                a_off = jnp.where(in_block, _dot_nt(qt, k_off[i - 1]), a_off)
            o = o + _dot(a_off.astype(BF16), vb)
            for d in range(GLA_SUB):
                a_d = jnp.sum(jnp.where(mine, diag[d], 0.0), axis=1, keepdims=True)
                o = o + a_d * (v if d == 0 else pltpu.roll(v, d, axis=0))
            o_ref[:, GLA_DV * head:GLA_DV * (head + 1)] = o
            s_new = s_new + lax.dot_general(jnp.where(mine, k_dec, 0.0).astype(BF16), vb, (((0,), (0,)), ((), ())),
                                            preferred_element_type=F32)
        s_ref[slab] = s_new

    @pl.when(c == pl.num_programs(1) - 1)
    def _():
        sfin_ref[0] = s_ref[...]


def _gla_prompt(qg, kg, la, vg, batch):
    t = qg.shape[0]
    nc = t // batch // GLA_CHUNK
    n_slab = GLA_HEADS // 2
    row = lambda n: pl.BlockSpec((GLA_CHUNK, n), lambda b, c: (b * nc + c, 0))
    o, s_fin = pl.pallas_call(
        _gla_prompt_kernel,
        grid=(batch, nc),
        in_specs=[row(256), row(256), row(256), row(512)],
        out_specs=[row(512), pl.BlockSpec((1, n_slab, 2 * GLA_DK, GLA_DV), lambda b, c: (b, 0, 0, 0))],
        out_shape=[jax.ShapeDtypeStruct((t, GLA_V_W), F32),
                   jax.ShapeDtypeStruct((batch, n_slab, 2 * GLA_DK, GLA_DV), F32)],
        scratch_shapes=[pltpu.VMEM((n_slab, 2 * GLA_DK, GLA_DV), F32)],
        compiler_params=_cparams(("parallel", "arbitrary")),
        name="gla_prompt",
    )(qg, kg, la, vg)
    return o, s_fin.reshape(batch, GLA_HEADS, GLA_DK, GLA_DV)


GLA_DEC_BLOCK = 16


def _gla_sample_kernel(qt_ref, kt_ref, lat_ref, v_ref, s_ref, o_ref, snew_ref):
    for i in range(GLA_DEC_BLOCK):
        for h in range(GLA_HEADS):
            rows = slice(GLA_DK * h, GLA_DK * (h + 1))
            qc = qt_ref[0, rows, i:i + 1]
            kc = kt_ref[0, rows, i:i + 1]
            ac = jnp.exp(lat_ref[0, rows, i:i + 1])
            vr = v_ref[i:i + 1, GLA_DV * h:GLA_DV * (h + 1)]
            s_new = ac * s_ref[i, h] + kc * vr
            snew_ref[i, h] = s_new
            o_ref[i:i + 1, GLA_DV * h:GLA_DV * (h + 1)] = jnp.sum(qc * s_new, axis=0, keepdims=True)


def _gla_sample(qg, kg, la, vg, states, layer):
    n = qg.shape[0]
    nb = n // GLA_DEC_BLOCK
    cols = lambda a: a.reshape(nb, GLA_DEC_BLOCK, GLA_QK_W).transpose(0, 2, 1)
    colspec = pl.BlockSpec((1, GLA_QK_W, GLA_DEC_BLOCK), lambda i: (i, 0, 0))
    sblock = (GLA_DEC_BLOCK, GLA_HEADS, GLA_DK, GLA_DV)
    return pl.pallas_call(
        _gla_sample_kernel,
        grid=(nb,),
        in_specs=[colspec, colspec, colspec, pl.BlockSpec((GLA_DEC_BLOCK, GLA_V_W), lambda i: (i, 0)),
                  pl.BlockSpec((None,) + sblock, lambda i: (layer, i, 0, 0, 0))],
        out_specs=[pl.BlockSpec((GLA_DEC_BLOCK, GLA_V_W), lambda i: (i, 0)),
                   pl.BlockSpec(sblock, lambda i: (i, 0, 0, 0))],
        out_shape=[jax.ShapeDtypeStruct((n, GLA_V_W), F32), jax.ShapeDtypeStruct(states.shape[1:], F32)],
        compiler_params=_cparams(("parallel",)),
        name="gla_sample",
    )(cols(qg), cols(kg), cols(la), vg, states)


FLASH_TQ = 256
FLASH_TK = 1024


def _causal_flash(get_q, n_groups, qi, k_ref, get_v, m_ref, l_ref, acc_ref):
    tq, tk = FLASH_TQ, FLASH_TK
    m_ref[...] = jnp.full_like(m_ref, NEG_BIG)
    acc_ref[...] = jnp.zeros_like(acc_ref)
    if l_ref is not None:
        l_ref[...] = jnp.zeros_like(l_ref)

    def chunk(start, masked):
        k = k_ref[pl.ds(start, tk), :]
        v = get_v(k, start)
        if masked:
            tok = qi * tq + lax.broadcasted_iota(jnp.int32, (tq, tk), 0)
            key = start + lax.broadcasted_iota(jnp.int32, (tq, tk), 1)
            visible = key <= tok
        for g in range(n_groups):
            s = _dot_nt(get_q(g), k)
            if masked:
                s = jnp.where(visible, s, NEG_BIG)
            m_prev = m_ref[g]
            m_new = jnp.maximum(m_prev, jnp.max(s, axis=1, keepdims=True))
            alpha = jnp.exp2(m_prev - m_new)
            p = jnp.exp2(s - m_new)
            if l_ref is not None:
                l_ref[g] = alpha * l_ref[g] + jnp.sum(p, axis=1, keepdims=True)
            acc_ref[g] = alpha * acc_ref[g] + _dot(p.astype(BF16), v)
            m_ref[g] = m_new

    n_chunks = ((qi + 1) * tq + tk - 1) // tk

    def body(i, carry):
        chunk(pl.multiple_of(i * tk, tk), False)
        return carry

    lax.fori_loop(0, n_chunks - 1, body, 0)
    chunk(pl.multiple_of((n_chunks - 1) * tk, tk), True)


def _mla_prompt_kernel(q_ref, k_ref, o_ref, m_ref, l_ref, acc_ref):
    _causal_flash(lambda g: q_ref[g], MLA_HEADS, pl.program_id(1), k_ref, lambda k, start: k[:, :MLA_KV_RANK],
                  m_ref, l_ref, acc_ref)
    o_ref[...] = acc_ref[...] / l_ref[...]


def _mla_prompt(qcat, kcat, batch):
    t = kcat.shape[0]
    seq = t // batch
    tq = FLASH_TQ
    assert seq % FLASH_TK == 0
    nq = seq // tq
    return pl.pallas_call(
        _mla_prompt_kernel,
        grid=(batch, nq),
        in_specs=[pl.BlockSpec((MLA_HEADS, tq, MLA_QK_PAD), lambda b, i: (0, b * nq + i, 0)),
                  pl.BlockSpec((seq, MLA_QK_PAD), lambda b, i: (b, 0))],
        out_specs=pl.BlockSpec((MLA_HEADS, tq, MLA_KV_RANK), lambda b, i: (0, b * nq + i, 0)),
        out_shape=jax.ShapeDtypeStruct((MLA_HEADS, t, MLA_KV_RANK), F32),
        scratch_shapes=[pltpu.VMEM((MLA_HEADS, tq, 1), F32), pltpu.VMEM((MLA_HEADS, tq, 1), F32),
                        pltpu.VMEM((MLA_HEADS, tq, MLA_KV_RANK), F32)],
        compiler_params=_cparams(("parallel", "arbitrary")),
        name="mla_prompt",
    )(qcat, kcat)


MLA_DEC_PAGES = 64
DEC_ROWS = 8


def _mla_sample_kernel(pt_ref, q_ref, knew_ref, *refs, pp):
    ckv_refs = refs[:pp]
    krt_refs = refs[pp:2 * pp]
    o_ref, m_ref, l_ref, acc_ref = refs[2 * pp:]
    c = pl.program_id(1)

    @pl.when(c == 0)
    def _():
        m_ref[...] = jnp.full_like(m_ref, NEG_BIG)
        l_ref[...] = jnp.zeros_like(l_ref)
        acc_ref[...] = jnp.zeros_like(acc_ref)

    q = q_ref[0]
    q_lat = q[:, :MLA_KV_RANK]
    q_rope = q[:, MLA_KV_RANK:MLA_KV_RANK + MLA_ROPE_DIM]
    pages = [r[...].astype(BF16) for r in ckv_refs]
    s = jnp.concatenate(
        [_dot_nt(q_lat, pages[i]) + _dot(q_rope, krt_refs[i][...].astype(BF16)) for i in range(pp)], axis=1)
    m_prev = m_ref[...]
    m_new = jnp.maximum(m_prev, jnp.max(s, axis=1, keepdims=True))
    alpha = jnp.exp2(m_prev - m_new)
    p = jnp.exp2(s - m_new)
    l_ref[...] = alpha * l_ref[...] + jnp.sum(p, axis=1, keepdims=True)
    pb = p.astype(BF16)
    psz = pages[0].shape[0]
    pv = _dot(pb[:, :psz], pages[0])
    for i in range(1, pp):
        pv = pv + _dot(pb[:, psz * i:psz * (i + 1)], pages[i])
    acc_ref[...] = alpha * acc_ref[...] + pv
    m_ref[...] = m_new

    @pl.when(c == pl.num_programs(1) - 1)
    def _():
        knew = knew_ref[0]
        s_new = jnp.sum(q.astype(F32) * knew.astype(F32), axis=1, keepdims=True)
        m_old = m_ref[...]
        m_fin = jnp.maximum(m_old, s_new)
        a = jnp.exp2(m_old - m_fin)
        p_new = jnp.exp2(s_new - m_fin)
        l_fin = a * l_ref[...] + p_new
        v_new = knew[:, :MLA_KV_RANK].astype(F32)
        o_ref[0] = (a * acc_ref[...] + p_new.astype(BF16).astype(F32) * v_new) / l_fin


def _mla_sample(page_table, q, knew, cache_ckv, cache_krt, layer):
    n, n_pages = page_table.shape
    pp = min(MLA_DEC_PAGES, n_pages)
    assert n_pages % pp == 0
    page = cache_ckv.shape[2]

    def pool_spec(shape, i):
        return pl.BlockSpec((None, None) + shape, lambda b, c, pt: (layer, pt[b * n_pages + c * pp + i], 0, 0))

    grid_spec = pltpu.PrefetchScalarGridSpec(
        num_scalar_prefetch=1,
        grid=(n, n_pages // pp),
        in_specs=([pl.BlockSpec((1, DEC_ROWS, MLA_QK_PAD), lambda b, c, pt: (b, 0, 0)),
                   pl.BlockSpec((1, 1, MLA_QK_PAD), lambda b, c, pt: (b, 0, 0))]
                  + [pool_spec((page, MLA_KV_RANK), i) for i in range(pp)]
                  + [pool_spec((MLA_ROPE_DIM, page), i) for i in range(pp)]),
        out_specs=pl.BlockSpec((1, DEC_ROWS, MLA_KV_RANK), lambda b, c, pt: (b, 0, 0)),
        scratch_shapes=[pltpu.VMEM((DEC_ROWS, 1), F32), pltpu.VMEM((DEC_ROWS, 1), F32),
                        pltpu.VMEM((DEC_ROWS, MLA_KV_RANK), F32)],
    )
    return pl.pallas_call(
        functools.partial(_mla_sample_kernel, pp=pp),
        grid_spec=grid_spec,
        out_shape=jax.ShapeDtypeStruct((n, DEC_ROWS, MLA_KV_RANK), F32),
        compiler_params=_cparams(("parallel", "arbitrary")),
        name="mla_sample",
    )(page_table.reshape(-1), q, knew, *([cache_ckv] * pp), *([cache_krt] * pp))


def _even_mix_kernel(y_ref, og_ref, rg_ref, olat_ref, glanorm_ref, wuv_ref, wout_ref, gpost_ref, out_ref):
    pieces = []
    for h in range(GLA_HEADS):
        seg = slice(GLA_DV * h, GLA_DV * (h + 1))
        pieces.append((_rms(og_ref[:, seg], glanorm_ref[...]) * _silu(rg_ref[:, seg])).astype(BF16))
    for h in range(MLA_HEADS):
        pieces.append(_dot(olat_ref[h].astype(BF16), wuv_ref[h]).astype(BF16))
    mix = jnp.concatenate(pieces, axis=1)
    out_ref[...] = y_ref[...] + _rms(_dot(mix, wout_ref[...]), gpost_ref[...])


def _even_mix(y, og, rg, olat, w, tm):
    t = y.shape[0]
    row = lambda n: pl.BlockSpec((tm, n), lambda i: (i, 0))
    full2 = lambda a: pl.BlockSpec(a.shape, lambda i: (0, 0))
    full3 = lambda a: pl.BlockSpec(a.shape, lambda i: (0, 0, 0))
    return pl.pallas_call(
        _even_mix_kernel,
        grid=(t // tm,),
        in_specs=[row(D_MODEL), row(GLA_V_W), row(GLA_V_W),
                  pl.BlockSpec((MLA_HEADS, tm, MLA_KV_RANK), lambda i: (0, i, 0)),
                  full2(w["gla_norm"]), full3(w["w_uv"]), full2(w["w_out"]), full2(w["g_post"])],
        out_specs=row(D_MODEL),
        out_shape=jax.ShapeDtypeStruct((t, D_MODEL), F32),
        compiler_params=_cparams(("parallel",)),
        name="even_mix",
    )(y, og, rg, olat, w["gla_norm"], w["w_uv"], w["w_out"], w["g_post"])


def _odd_proj_kernel(y_ref, cos_ref, sin_ref, gpre_ref, win_ref, dk_ref, dv_ref, qd_ref, kd_ref, vd_ref):
    h = _rms(y_ref[...], gpre_ref[...]).astype(BF16)
    p = _dot(h, win_ref[...])
    cos = cos_ref[...]
    sin = sin_ref[...]
    lane = lax.broadcasted_iota(jnp.int32, cos.shape, 1)
    low = lane < DIFF_DH
    sm = DIFF_DH ** -0.5 * LOG2E
    for head in range(DIFF_HEADS):
        g, r = divmod(head, DIFF_REP)
        slab = _rope128(p[:, LANES * head:LANES * (head + 1)], cos, sin) * sm
        qd_ref[g, r] = jnp.where(low, slab, 0.0).astype(BF16)
        qd_ref[g, DIFF_REP + r] = jnp.where(low, 0.0, slab).astype(BF16)
    for g in range(DIFF_KV_HEADS):
        seg = slice(LANES * g, LANES * (g + 1))
        kk = _rope128(p[:, DIFF_Q_W + LANES * g:DIFF_Q_W + LANES * (g + 1)], cos, sin)
        dk_ref[:, seg] = kk
        kd_ref[:, seg] = kk.astype(BF16)
    vv = p[:, DIFF_Q_W + DIFF_K_W:ODD_IN]
    dv_ref[...] = vv
    vd_ref[...] = vv.astype(BF16)


def _odd_proj(y, cos, sin, w, tm):
    t = y.shape[0]
    row = lambda n: pl.BlockSpec((tm, n), lambda i: (i, 0))
    full2 = lambda a: pl.BlockSpec(a.shape, lambda i: (0, 0))
    return pl.pallas_call(
        _odd_proj_kernel,
        grid=(t // tm,),
        in_specs=[row(D_MODEL), _table_spec(cos, tm), _table_spec(cos, tm), full2(w["g_pre"]), full2(w["w_in"])],
        out_specs=[row(DIFF_K_W), row(DIFF_V_W),
                   pl.BlockSpec((DIFF_KV_HEADS, 2 * DIFF_REP, tm, LANES), lambda i: (0, 0, i, 0)),
                   row(DIFF_K_W), row(DIFF_V_W)],
        out_shape=[jax.ShapeDtypeStruct((t, DIFF_K_W), F32), jax.ShapeDtypeStruct((t, DIFF_V_W), F32),
                   jax.ShapeDtypeStruct((DIFF_KV_HEADS, 2 * DIFF_REP, t, LANES), BF16),
                   jax.ShapeDtypeStruct((t, DIFF_K_W), BF16), jax.ShapeDtypeStruct((t, DIFF_V_W), BF16)],
        compiler_params=_cparams(("parallel",)),
        name="odd_proj",
    )(y, cos, sin, w["g_pre"], w["w_in"])


def _diff_lambda(lam_ref, lam_init):
    lq1, lk1, lq2, lk2 = lam_ref[0:1, :], lam_ref[1:2, :], lam_ref[2:3, :], lam_ref[3:4, :]
    return (jnp.exp(jnp.sum(lq1 * lk1, axis=1, keepdims=True))
            - jnp.exp(jnp.sum(lq2 * lk2, axis=1, keepdims=True)) + lam_init)


def _diff_prompt_kernel(lam_ref, q_ref, k_ref, v_ref, o_ref, m_ref, acc_ref, *, lam_init):
    def values(k, start):
        v = v_ref[pl.ds(start, FLASH_TK), :]
        return jnp.concatenate([v, jnp.ones_like(v)], axis=1)

    _causal_flash(lambda g: q_ref[0, g], 2 * DIFF_REP, pl.program_id(2), k_ref, values, m_ref, None, acc_ref)
    lam = _diff_lambda(lam_ref, lam_init)
    for r in range(DIFF_REP):
        a0 = acc_ref[r]
        a1 = acc_ref[DIFF_REP + r]
        o_ref[:, DIFF_DV * r:DIFF_DV * (r + 1)] = (
            a0[:, :DIFF_DV] / a0[:, DIFF_DV:] - lam * (a1[:, :DIFF_DV] / a1[:, DIFF_DV:]))


def _diff_prompt(qd, kd, vd, lam_vecs, lam_init, batch):
    t = kd.shape[0]
    seq = t // batch
    tq = FLASH_TQ
    assert seq % FLASH_TK == 0
    nq = seq // tq
    groups = 2 * DIFF_REP
    return pl.pallas_call(
        functools.partial(_diff_prompt_kernel, lam_init=lam_init),
        grid=(batch, DIFF_KV_HEADS, nq),
        in_specs=[pl.BlockSpec(lam_vecs.shape, lambda b, g, i: (0, 0)),
                  pl.BlockSpec((1, 2 * DIFF_REP, tq, LANES), lambda b, g, i: (g, 0, b * nq + i, 0)),
                  pl.BlockSpec((seq, LANES), lambda b, g, i: (b, g)),
                  pl.BlockSpec((seq, DIFF_DV), lambda b, g, i: (b, g))],
        out_specs=pl.BlockSpec((tq, DIFF_REP * DIFF_DV), lambda b, g, i: (b * nq + i, g)),
        out_shape=jax.ShapeDtypeStruct((t, DIFF_HEADS * DIFF_DV), F32),
        scratch_shapes=[pltpu.VMEM((groups, tq, 1), F32), pltpu.VMEM((groups, tq, 2 * DIFF_DV), F32)],
        compiler_params=_cparams(("parallel", "parallel", "arbitrary")),
        name="diff_prompt",
    )(lam_vecs, qd, kd, vd)


DIFF_DEC_ROWS = 2 * DIFF_HEADS
DIFF_DEC_PAGES = 64


def _diff_sample_kernel(pt_ref, lam_ref, q_ref, knew_ref, vnew_ref, *refs, lam_init, pp):
    k_refs = refs[:pp]
    v_refs = refs[pp:2 * pp]
    o_ref, m_ref, l_ref, acc_ref = refs[2 * pp:]
    c = pl.program_id(1)

    @pl.when(c == 0)
    def _():
        m_ref[...] = jnp.full_like(m_ref, NEG_BIG)
        l_ref[...] = jnp.zeros_like(l_ref)
        acc_ref[...] = jnp.zeros_like(acc_ref)

    q = q_ref[0]
    s = jnp.concatenate([_dot_nt(q, k_refs[i][...].astype(BF16)) for i in range(pp)], axis=1)
    row_head = lax.broadcasted_iota(jnp.int32, s.shape, 0) // (2 * DIFF_REP)
    col_head = lax.broadcasted_iota(jnp.int32, s.shape, 1) & (DIFF_KV_HEADS - 1)
    s = jnp.where(row_head == col_head, s, NEG_BIG)
    m_prev = m_ref[...]
    m_new = jnp.maximum(m_prev, jnp.max(s, axis=1, keepdims=True))
    alpha = jnp.exp2(m_prev - m_new)
    p = jnp.exp2(s - m_new)
    l_ref[...] = alpha * l_ref[...] + jnp.sum(p, axis=1, keepdims=True)
    pb = p.astype(BF16)
    psz = k_refs[0].shape[0]
    pv = _dot(pb[:, :psz], v_refs[0][...].astype(BF16))
    for i in range(1, pp):
        pv = pv + _dot(pb[:, psz * i:psz * (i + 1)], v_refs[i][...].astype(BF16))
    acc_ref[...] = alpha * acc_ref[...] + pv
    m_ref[...] = m_new

    @pl.when(c == pl.num_programs(1) - 1)
    def _():
        first = lax.broadcasted_iota(jnp.int32, (DIFF_DEC_ROWS, LANES), 0) < 2 * DIFF_REP
        knew = knew_ref[0].astype(F32)
        vnew = vnew_ref[0].astype(F32)
        k_row = jnp.where(first, knew[:, :LANES], knew[:, LANES:])
        v_row = jnp.where(first, vnew[:, :DIFF_DV], vnew[:, DIFF_DV:])
        s_new = jnp.sum(q.astype(F32) * k_row, axis=1, keepdims=True)
        m_old = m_ref[...]
        m_fin = jnp.maximum(m_old, s_new)
        a = jnp.exp2(m_old - m_fin)
        p_new = jnp.exp2(s_new - m_fin)
        l_fin = a * l_ref[...] + p_new
        o = (a * acc_ref[...] + p_new.astype(BF16).astype(F32) * v_row) / l_fin
        lam = _diff_lambda(lam_ref, lam_init)
        for g in range(DIFF_KV_HEADS):
            base = 2 * DIFF_REP * g
            o_ref[0, DIFF_REP * g:DIFF_REP * (g + 1), :] = (
                o[base:base + DIFF_REP] - lam * o[base + DIFF_REP:base + 2 * DIFF_REP])


def _diff_sample(page_table, lam_vecs, lam_init, q, knew, vnew, cache_k, cache_v, layer):
    n, n_pages = page_table.shape
    pp = min(DIFF_DEC_PAGES, n_pages)
    assert n_pages % pp == 0
    page_rows = cache_k.shape[2]

    def pool_spec(i):
        return pl.BlockSpec((None, None, page_rows, LANES),
                            lambda b, c, pt: (layer, pt[b * n_pages + c * pp + i], 0, 0))

    grid_spec = pltpu.PrefetchScalarGridSpec(
        num_scalar_prefetch=1,
        grid=(n, n_pages // pp),
        in_specs=([pl.BlockSpec(lam_vecs.shape, lambda b, c, pt: (0, 0)),
                   pl.BlockSpec((1, DIFF_DEC_ROWS, LANES), lambda b, c, pt: (b, 0, 0)),
                   pl.BlockSpec((1, 1, DIFF_K_W), lambda b, c, pt: (b, 0, 0)),
                   pl.BlockSpec((1, 1, DIFF_V_W), lambda b, c, pt: (b, 0, 0))]
                  + [pool_spec(i) for i in range(pp)] + [pool_spec(i) for i in range(pp)]),
        out_specs=pl.BlockSpec((1, DIFF_HEADS, DIFF_DV), lambda b, c, pt: (b, 0, 0)),
        scratch_shapes=[pltpu.VMEM((DIFF_DEC_ROWS, 1), F32), pltpu.VMEM((DIFF_DEC_ROWS, 1), F32),
                        pltpu.VMEM((DIFF_DEC_ROWS, DIFF_DV), F32)],
    )
    return pl.pallas_call(
        functools.partial(_diff_sample_kernel, lam_init=lam_init, pp=pp),
        grid_spec=grid_spec,
        out_shape=jax.ShapeDtypeStruct((n, DIFF_HEADS, DIFF_DV), F32),
        compiler_params=_cparams(("parallel", "arbitrary")),
        name="diff_sample",
    )(page_table.reshape(-1), lam_vecs, q, knew, vnew, *([cache_k] * pp), *([cache_v] * pp))


def _odd_mix_kernel(y_ref, o_ref, subln_ref, wout_ref, gpost_ref, out_ref, *, lam_init):
    pieces = []
    for h in range(DIFF_HEADS):
        seg = slice(DIFF_DV * h, DIFF_DV * (h + 1))
        pieces.append((_rms(o_ref[:, seg], subln_ref[...]) * (1.0 - lam_init)).astype(BF16))
    mix = jnp.concatenate(pieces, axis=1)
    out_ref[...] = y_ref[...] + _rms(_dot(mix, wout_ref[...]), gpost_ref[...])


def _odd_mix(y, o, w, lam_init, tm):
    t = y.shape[0]
    row = lambda n: pl.BlockSpec((tm, n), lambda i: (i, 0))
    full2 = lambda a: pl.BlockSpec(a.shape, lambda i: (0, 0))
    return pl.pallas_call(
        functools.partial(_odd_mix_kernel, lam_init=lam_init),
        grid=(t // tm,),
        in_specs=[row(D_MODEL), row(D_MODEL), full2(w["subln"]), full2(w["w_out"]), full2(w["g_post"])],
        out_specs=row(D_MODEL),
        out_shape=jax.ShapeDtypeStruct((t, D_MODEL), F32),
        compiler_params=_cparams(("parallel",)),
        name="odd_mix",
    )(y, o, w["subln"], w["w_out"], w["g_post"])


FFN_SPLIT = 2


def _ffn_kernel(y_ref, gpre_ref, wg_ref, wu_ref, wd_ref, gpost_ref, out_ref, h_ref, acc_ref):
    h_ref[...] = _rms(y_ref[...], gpre_ref[...]).astype(BF16)
    h = h_ref[...]
    w = wg_ref.shape[1] // FFN_SPLIT
    for s in range(FFN_SPLIT):
        cols = slice(w * s, w * (s + 1))
        a = (_silu(_dot(h, wg_ref[:, cols])) * _dot(h, wu_ref[:, cols])).astype(BF16)
        part = _dot(a, wd_ref[cols, :])
        if s == 0:
            acc_ref[...] = part
        else:
            acc_ref[...] += part
    out_ref[...] = y_ref[...] + _rms(acc_ref[...], gpost_ref[...])


def _ffn(y, w, tm):
    t = y.shape[0]
    d_ff = w["w_gate"].shape[1]
    assert d_ff % (FFN_SPLIT * LANES) == 0
    const = lambda i: (0, 0)
    return pl.pallas_call(
        _ffn_kernel,
        grid=(t // tm,),
        in_specs=[pl.BlockSpec((tm, D_MODEL), lambda i: (i, 0)),
                  pl.BlockSpec((1, D_MODEL), const),
                  pl.BlockSpec((D_MODEL, d_ff), const),
                  pl.BlockSpec((D_MODEL, d_ff), const),
                  pl.BlockSpec((d_ff, D_MODEL), const),
                  pl.BlockSpec((1, D_MODEL), const)],
        out_specs=pl.BlockSpec((tm, D_MODEL), lambda i: (i, 0)),
        out_shape=jax.ShapeDtypeStruct((t, D_MODEL), F32),
        scratch_shapes=[pltpu.VMEM((tm, D_MODEL), BF16), pltpu.VMEM((tm, D_MODEL), F32)],
        compiler_params=_cparams(("parallel",)),
        name="ffn",
    )(y, w["g_pre"], w["w_gate"], w["w_up"], w["w_down"], w["g_post"])


def _rope_tables(pos):
    half = MLA_ROPE_DIM // 2
    inv = ROPE_THETA ** (-jnp.arange(0, MLA_ROPE_DIM, 2, dtype=F32) / MLA_ROPE_DIM)
    ang = pos.astype(F32)[:, None] * inv[None, :]
    cos, sin = jnp.cos(ang), jnp.sin(ang)
    assert cos.shape[1] == half
    return jnp.tile(cos, (1, 4)), jnp.tile(jnp.concatenate([-sin, sin], axis=1), (1, 2))


def _row(v):
    return v.reshape(1, -1).astype(F32)


def _pack_even(j, norm_mix_pre_i, norm_mix_post_i, w_in_even, w_gla_alpha, b_gla_alpha, gla_out_norm, mla_q_norm,
               w_mla_uq, mla_kv_norm, w_mla_uk, w_mla_uv, w_out_even):
    w_in = w_in_even[j]
    widths = [GLA_QK_W, GLA_QK_W, GLA_V_W, GLA_GATE_RANK, GLA_V_W, MLA_Q_RANK, MLA_KV_RANK, MLA_ROPE_DIM]
    offs = np.concatenate([[0], np.cumsum(widths)])
    seg = lambda n: w_in[:, int(offs[n]):int(offs[n + 1])]
    pad = jnp.zeros((D_MODEL, EVEN_IN_PAD - int(offs[-1])), w_in.dtype)
    packed = jnp.concatenate([seg(0), seg(1), seg(2), seg(4), seg(5), seg(6), seg(7), seg(3), pad], axis=1)
    w_alpha = jnp.zeros((LANES, GLA_QK_W), F32).at[MLA_ROPE_DIM:MLA_ROPE_DIM + GLA_GATE_RANK].set(w_gla_alpha[j])
    uq = w_mla_uq[j].reshape(MLA_Q_RANK, MLA_HEADS, MLA_NOPE_DIM + MLA_ROPE_DIM)
    uq = jnp.concatenate([uq[:, :, :MLA_NOPE_DIM].reshape(MLA_Q_RANK, -1),
                          uq[:, :, MLA_NOPE_DIM:].reshape(MLA_Q_RANK, -1)], axis=1)
    return {
        "g_pre": _row(norm_mix_pre_i), "g_post": _row(norm_mix_post_i),
        "w_in": packed.astype(BF16), "w_alpha": w_alpha.astype(BF16), "b_alpha": _row(b_gla_alpha[j]),
        "q_norm": _row(mla_q_norm[j]), "kv_norm": _row(mla_kv_norm[j]), "w_uq": uq.astype(BF16),
        "w_ukt": jnp.transpose(w_mla_uk[j], (1, 2, 0)).astype(BF16),
        "w_uv": jnp.transpose(w_mla_uv[j], (1, 0, 2)).astype(BF16),
        "gla_norm": _row(gla_out_norm[j]), "w_out": w_out_even[j].astype(BF16),
    }


def _decode_q_mla(qcat):
    q = jnp.transpose(qcat, (1, 0, 2))
    return jnp.pad(q, ((0, 0), (0, DEC_ROWS - MLA_HEADS), (0, 0)))


def _decode_q_diff(qd):
    n = qd.shape[2]
    return jnp.transpose(qd, (2, 0, 1, 3)).reshape(n, DIFF_DEC_ROWS, LANES)


def kernel(x_prompt, x_sample, state_gla, cache_mla_ckv, cache_mla_krope, cache_diff_k, cache_diff_v, page_table,
           norm_mix_pre, norm_mix_post, norm_ffn_pre, norm_ffn_post,
           w_in_even, w_gla_alpha, b_gla_alpha, gla_out_norm, mla_q_norm, w_mla_uq, mla_kv_norm, w_mla_uk, w_mla_uv,
           w_out_even,
           w_in_odd, diff_lambda_q1, diff_lambda_k1, diff_lambda_q2, diff_lambda_k2, diff_subln, w_out_odd,
           w_ffn_gate, w_ffn_up, w_ffn_down):
    batch, seq, _ = x_prompt.shape
    n_dec, dec_seq, _ = x_sample.shape
    assert dec_seq == 1
    depth = norm_mix_pre.shape[0]
    n_pages = page_table.shape[1]
    page = cache_mla_ckv.shape[2]
    past_len = n_pages * page
    tp = batch * seq
    tm_p = min(1024, seq)
    tm_f = min(512, seq)
    tm_s = n_dec

    cos_p, sin_p = _rope_tables(jnp.arange(seq, dtype=jnp.int32))
    cos_s, sin_s = _rope_tables(jnp.full((n_dec,), past_len, jnp.int32))

    cache_krt = jnp.transpose(cache_mla_krope, (0, 1, 3, 2))
    cache_dk = cache_diff_k.reshape(cache_diff_k.shape[:2] + (page * DIFF_KV_HEADS, 2 * DIFF_DH))
    cache_dv = cache_diff_v.reshape(cache_diff_v.shape[:2] + (page * DIFF_KV_HEADS, DIFF_DV))

    yp = x_prompt.reshape(tp, D_MODEL)
    ys = x_sample.reshape(n_dec, D_MODEL)
    gla_p, gla_s, ckv_p, kr_p, ckv_s, kr_s = [], [], [], [], [], []
    dk_p, dv_p, dk_s, dv_s = [], [], [], []
    for i in range(depth):
        j = i // 2
        if i % 2 == 0:
            w = _pack_even(j, norm_mix_pre[i], norm_mix_post[i], w_in_even, w_gla_alpha, b_gla_alpha, gla_out_norm,
                           mla_q_norm, w_mla_uq, mla_kv_norm, w_mla_uk, w_mla_uv, w_out_even)
            qg, kg, vg, rg, la, ckv, krope, qcat, kcat = _even_proj(yp, cos_p, sin_p, w, tm_p)
            og, s_fin = _gla_prompt(qg, kg, la, vg, batch)
            olat = _mla_prompt(qcat, kcat, batch)
            yp = _even_mix(yp, og, rg, olat, w, tm_p)
            gla_p.append(s_fin)
            ckv_p.append(ckv.reshape(batch, seq, MLA_KV_RANK))
            kr_p.append(krope.reshape(batch, seq, MLA_ROPE_DIM))

            qg, kg, vg, rg, la, ckv, krope, qcat, kcat = _even_proj(ys, cos_s, sin_s, w, tm_s)
            og, s_new = _gla_sample(qg, kg, la, vg, state_gla, j)
            olat = _mla_sample(page_table, _decode_q_mla(qcat), kcat.reshape(n_dec, 1, MLA_QK_PAD),
                               cache_mla_ckv, cache_krt, j)
            olat = jnp.transpose(olat[:, :MLA_HEADS], (1, 0, 2))
            ys = _even_mix(ys, og, rg, olat, w, tm_s)
            gla_s.append(s_new)
            ckv_s.append(ckv.reshape(n_dec, 1, MLA_KV_RANK))
            kr_s.append(krope.reshape(n_dec, 1, MLA_ROPE_DIM))
        else:
            lam_init = 0.8 - 0.6 * math.exp(-0.3 * i)
            w = {"g_pre": _row(norm_mix_pre[i]), "g_post": _row(norm_mix_post[i]), "w_in": w_in_odd[j].astype(BF16),
                 "subln": _row(diff_subln[j]), "w_out": w_out_odd[j].astype(BF16)}
            lam_vecs = jnp.stack([diff_lambda_q1[j], diff_lambda_k1[j], diff_lambda_q2[j], diff_lambda_k2[j]]).astype(F32)
            dk, dv, qd, kd, vd = _odd_proj(yp, cos_p, sin_p, w, tm_p)
            o = _diff_prompt(qd, kd, vd, lam_vecs, lam_init, batch)
            yp = _odd_mix(yp, o, w, lam_init, tm_p)
            dk_p.append(dk.reshape(batch, seq, DIFF_KV_HEADS, 2 * DIFF_DH))
            dv_p.append(dv.reshape(batch, seq, DIFF_KV_HEADS, DIFF_DV))

            dk, dv, qd, kd, vd = _odd_proj(ys, cos_s, sin_s, w, tm_s)
            o = _diff_sample(page_table, lam_vecs, lam_init, _decode_q_diff(qd), kd.reshape(n_dec, 1, DIFF_K_W),
                             vd.reshape(n_dec, 1, DIFF_V_W), cache_dk, cache_dv, j)
            ys = _odd_mix(ys, o.reshape(n_dec, DIFF_HEADS * DIFF_DV), w, lam_init, tm_s)
            dk_s.append(dk.reshape(n_dec, 1, DIFF_KV_HEADS, 2 * DIFF_DH))
            dv_s.append(dv.reshape(n_dec, 1, DIFF_KV_HEADS, DIFF_DV))
        wf = {"g_pre": _row(norm_ffn_pre[i]), "g_post": _row(norm_ffn_post[i]), "w_gate": w_ffn_gate[i].astype(BF16),
              "w_up": w_ffn_up[i].astype(BF16), "w_down": w_ffn_down[i].astype(BF16)}
        yp = _ffn(yp, wf, tm_f)
        ys = _ffn(ys, wf, tm_s)
    return (yp.reshape(batch, seq, D_MODEL), ys.reshape(n_dec, 1, D_MODEL),
            jnp.stack(gla_p), jnp.stack(gla_s),
            jnp.stack(ckv_p), jnp.stack(kr_p), jnp.stack(ckv_s), jnp.stack(kr_s),
            jnp.stack(dk_p), jnp.stack(dv_p), jnp.stack(dk_s), jnp.stack(dv_s))
```
